```python
import math
import jax, jax.numpy as jnp
from jax import lax
import numpy as np


D_MODEL = 4096
BATCH = 1
SEQ = 16384
DEPTH = 4

DIFF_HEADS = 8
DIFF_HEAD_DIM = 128
DIFF_WIDTH = DIFF_HEADS * 2 * DIFF_HEAD_DIM
Q_BLOCK = 128
REL_BUCKETS = 32
REL_MAX_DIST = 128
RET_HEADS = 8
RET_HEAD_DIM = 128
RET_WIDTH = RET_HEADS * RET_HEAD_DIM
RET_CHUNK = 128
S5_WIDTH = 1024
S5_GROUP = 16
S5_GROUPS = S5_WIDTH // S5_GROUP
S5_STATE = 64
N_BRANCHES = 3
GATE_RANK = 256
FFN_HIDDEN = ((8 * D_MODEL + 767) // 768) * 256
PLE_DIM = 256
EPS = 1e-6

OFF_DQ = 0
OFF_DK = OFF_DQ + DIFF_WIDTH
OFF_DV = OFF_DK + DIFF_WIDTH
OFF_RQ = OFF_DV + DIFF_WIDTH
OFF_RK = OFF_RQ + RET_WIDTH
OFF_RV = OFF_RK + RET_WIDTH
OFF_RG = OFF_RV + RET_WIDTH
OFF_SU = OFF_RG + RET_WIDTH
OFF_GATE = OFF_SU + S5_WIDTH
IN_WIDTH = OFF_GATE + GATE_RANK

kernel_name = "hybrid_diffattn_retention_s5_gated_trunk"


def rms_norm(x, gain):
    xf = x.astype(jnp.float32)
    y = xf * lax.rsqrt(jnp.mean(xf * xf, axis=-1, keepdims=True) + EPS)
    return (y * gain.astype(jnp.float32)).astype(x.dtype)


def t5_bucket(rel):
    n = jnp.maximum(rel, 0)
    max_exact = REL_BUCKETS // 2
    nf = jnp.maximum(n, 1).astype(jnp.float32)
    large = max_exact + (jnp.log(nf / max_exact) / math.log(REL_MAX_DIST / max_exact)
                         * (REL_BUCKETS - max_exact)).astype(jnp.int32)
    large = jnp.minimum(large, REL_BUCKETS - 1)
    return jnp.where(n < max_exact, n, large)


def diff_attention(q, k, v, rel_bias, lam, subln, lam_init):
    B, S = q.shape[0], q.shape[1]
    nb = S // Q_BLOCK
    scale = DIFF_HEAD_DIM ** -0.5
    k_pos = jnp.arange(S)
    q_blocks = q.reshape(B, nb, Q_BLOCK, DIFF_HEADS, 2, DIFF_HEAD_DIM).transpose(1, 0, 2, 3, 4, 5)

    def block(args):
        qb, bi = args
        q_pos = bi * Q_BLOCK + jnp.arange(Q_BLOCK)
        rel = q_pos[:, None] - k_pos[None, :]
        bias = jnp.transpose(rel_bias[t5_bucket(rel)], (2, 0, 1)).astype(jnp.float32)
        s = jnp.einsum('bqhcd,bkhcd->bchqk', qb, k).astype(jnp.float32) * scale + bias
        s = jnp.where(rel >= 0, s, -jnp.inf)
        pr = jax.nn.softmax(s, axis=-1)
        a = pr[:, 0] - lam * pr[:, 1]
        return jnp.einsum('bhqk,bkhe->bqhe', a.astype(v.dtype), v)

    out = lax.map(block, (q_blocks, jnp.arange(nb)))
    out = out.transpose(1, 0, 2, 3, 4).reshape(B, S, DIFF_HEADS, 2 * DIFF_HEAD_DIM)
    out = rms_norm(out, subln) * (1.0 - lam_init)
    return out.reshape(B, S, DIFF_WIDTH)


def retention(q, k, v, g):
    B, S = q.shape[0], q.shape[1]
    f32 = jnp.float32
    half = RET_HEAD_DIM // 2
    pos = jnp.arange(S, dtype=f32)
    theta = 1.0 / (10000.0 ** jnp.linspace(0.0, 1.0, half, dtype=f32))
    ang = pos[:, None] * theta[None, :]
    cos = jnp.cos(ang)[None, :, None, :]
    sin = jnp.sin(ang)[None, :, None, :]

    def rot(t):
        t1, t2 = t[..., :half], t[..., half:]
        return jnp.concatenate([t1 * cos - t2 * sin, t1 * sin + t2 * cos], axis=-1)

    q = rot(q.astype(f32))
    k = rot(k.astype(f32)) * (RET_HEAD_DIM ** -0.5)
    v = v.astype(f32)

    log_gamma = jnp.log1p(-jnp.exp2(-5.0 - jnp.arange(RET_HEADS, dtype=f32)))
    idx = jnp.arange(RET_CHUNK)
    diff = idx[:, None] - idx[None, :]
    dmat = jnp.where(diff >= 0,
                     jnp.exp(log_gamma[:, None, None] * jnp.maximum(diff, 0).astype(f32)), 0.0)
    xi = jnp.exp(log_gamma[None, :] * (idx[:, None] + 1).astype(f32))
    zeta = jnp.exp(log_gamma[None, :] * (RET_CHUNK - 1 - idx)[:, None].astype(f32))
    chunk_decay = jnp.exp(log_gamma * RET_CHUNK)

    nc = S // RET_CHUNK

    def chunks(t):
        return t.reshape(B, nc, RET_CHUNK, RET_HEADS, RET_HEAD_DIM).transpose(1, 0, 2, 3, 4)

    def step(R, xs):
        qc, kc, vc = xs
        inner = jnp.einsum('bihd,bjhd->bhij', qc, kc) * dmat[None]
        o = (jnp.einsum('bhij,bjhe->bihe', inner, vc)
             + jnp.einsum('bihd,bhde->bihe', qc, R) * xi[None, :, :, None])
        R = (R * chunk_decay[None, :, None, None]
             + jnp.einsum('bjhd,bjhe->bhde', kc * zeta[None, :, :, None], vc))
        return R, o

    R0 = jnp.zeros((B, RET_HEADS, RET_HEAD_DIM, RET_HEAD_DIM), f32)
    _, o = lax.scan(step, R0, (chunks(q), chunks(k), chunks(v)))
    o = o.transpose(1, 0, 2, 3, 4).reshape(B, S, RET_HEADS, RET_HEAD_DIM)
    o = o * lax.rsqrt(jnp.mean(o * o, axis=-1, keepdims=True) + EPS)
    return jax.nn.silu(g.astype(f32)) * o.reshape(B, S, RET_WIDTH)


def s5_scan_op(e1, e2):
    a1r, a1i, b1r, b1i = e1
    a2r, a2i, b2r, b2i = e2
    return (a1r * a2r - a1i * a2i,
            a1r * a2i + a1i * a2r,
            a2r * b1r - a2i * b1i + b2r,
            a2r * b1i + a2i * b1r + b2i)


def s5_layer(u, lam_re, lam_im, log_dt, b_re, b_im, c_re, c_im, d_skip, w_glu):
    B, S = u.shape[0], u.shape[1]
    f32 = jnp.float32
    uf = u.astype(f32).reshape(B, S, S5_GROUPS, S5_GROUP)
    lr, li = lam_re.astype(f32), lam_im.astype(f32)
    dt = jnp.exp(log_dt.astype(f32))[:, None]
    mag = jnp.exp(lr * dt)
    ar = mag * jnp.cos(li * dt)
    ai = mag * jnp.sin(li * dt)
    den = lr * lr + li * li
    fr = ((ar - 1.0) * lr + ai * li) / den
    fi = (ai * lr - (ar - 1.0) * li) / den
    br_, bi_ = b_re.astype(f32), b_im.astype(f32)
    bbr = fr[..., None] * br_ - fi[..., None] * bi_
    bbi = fr[..., None] * bi_ + fi[..., None] * br_
    xr = jnp.einsum('bsgj,gnj->bsgn', uf, bbr)
    xim = jnp.einsum('bsgj,gnj->bsgn', uf, bbi)
    ar_t = jnp.broadcast_to(ar, xr.shape)
    ai_t = jnp.broadcast_to(ai, xr.shape)
    _, _, hr, hi = lax.associative_scan(s5_scan_op, (ar_t, ai_t, xr, xim), axis=1)
    y = (jnp.einsum('bsgn,gjn->bsgj', hr, c_re.astype(f32))
         - jnp.einsum('bsgn,gjn->bsgj', hi, c_im.astype(f32))
         + d_skip.astype(f32).reshape(S5_GROUPS, S5_GROUP) * uf)
    z = jax.nn.gelu(y.reshape(B, S, S5_WIDTH))
    zg = z @ w_glu.astype(f32)
    return zg[..., :S5_WIDTH] * jax.nn.sigmoid(zg[..., S5_WIDTH:])


def setup_inputs(seed: int = 0) -> dict:
    key = jax.random.key(seed)
    ks = jax.random.split(key, 32)
    f32 = jnp.float32

    def nrm(k, shape, scale):
        return jax.random.normal(k, shape, f32) * scale

    def gain(k, shape):
        return 1.0 + 0.01 * jax.random.normal(k, shape, f32)

    n_idx = jnp.arange(S5_STATE, dtype=f32)
    return {
        "x": nrm(ks[0], (BATCH, SEQ, D_MODEL), 1.0),
        "p": nrm(ks[1], (DEPTH, BATCH, SEQ, PLE_DIM), 1.0),
        "rel_bias": nrm(ks[2], (REL_BUCKETS, DIFF_HEADS), 0.1),
        "norm_mix": gain(ks[3], (DEPTH, D_MODEL)),
        "w_in": nrm(ks[4], (DEPTH, D_MODEL, IN_WIDTH), D_MODEL ** -0.5),
        "diff_lambda": nrm(ks[5], (DEPTH, 4, DIFF_HEAD_DIM), 0.1),
        "diff_subln": gain(ks[6], (DEPTH, 2 * DIFF_HEAD_DIM)),
        "s5_lambda_re": -0.5 + nrm(ks[7], (DEPTH, S5_GROUPS, S5_STATE), 0.01),
        "s5_lambda_im": jnp.pi * n_idx + nrm(ks[8], (DEPTH, S5_GROUPS, S5_STATE), 0.01),
        "s5_log_dt": jax.random.uniform(ks[9], (DEPTH, S5_GROUPS), f32,
                                        minval=math.log(0.001), maxval=math.log(0.1)),
        "s5_b_re": nrm(ks[10], (DEPTH, S5_GROUPS, S5_STATE, S5_GROUP), (2 * S5_GROUP) ** -0.5),
        "s5_b_im": nrm(ks[11], (DEPTH, S5_GROUPS, S5_STATE, S5_GROUP), (2 * S5_GROUP) ** -0.5),
        "s5_c_re": nrm(ks[12], (DEPTH, S5_GROUPS, S5_GROUP, S5_STATE), (2 * S5_STATE) ** -0.5),
        "s5_c_im": nrm(ks[13], (DEPTH, S5_GROUPS, S5_GROUP, S5_STATE), (2 * S5_STATE) ** -0.5),
        "s5_d": nrm(ks[14], (DEPTH, S5_WIDTH), 1.0),
        "s5_w_glu": nrm(ks[15], (DEPTH, S5_WIDTH, 2 * S5_WIDTH), S5_WIDTH ** -0.5),
        "w_gate_up": nrm(ks[16], (DEPTH, GATE_RANK, N_BRANCHES * D_MODEL), GATE_RANK ** -0.5),
        "w_br_diff": nrm(ks[17], (DEPTH, DIFF_WIDTH, D_MODEL), DIFF_WIDTH ** -0.5),
        "w_br_ret": nrm(ks[18], (DEPTH, RET_WIDTH, D_MODEL), RET_WIDTH ** -0.5),
        "w_br_s5": nrm(ks[19], (DEPTH, S5_WIDTH, D_MODEL), S5_WIDTH ** -0.5),
        "w_o": nrm(ks[20], (DEPTH, D_MODEL, D_MODEL), D_MODEL ** -0.5),
        "norm_ffn": gain(ks[21], (DEPTH, D_MODEL)),
        "w_ffn_gate": nrm(ks[22], (DEPTH, D_MODEL, FFN_HIDDEN), D_MODEL ** -0.5),
        "w_ffn_up": nrm(ks[23], (DEPTH, D_MODEL, FFN_HIDDEN), D_MODEL ** -0.5),
        "w_ffn_down": nrm(ks[24], (DEPTH, FFN_HIDDEN, D_MODEL), FFN_HIDDEN ** -0.5),
        "norm_ple": gain(ks[25], (DEPTH, D_MODEL)),
        "w_ple": nrm(ks[26], (DEPTH, PLE_DIM, D_MODEL), PLE_DIM ** -0.5),
        "w_ple_gate_down": nrm(ks[27], (DEPTH, D_MODEL, GATE_RANK), D_MODEL ** -0.5),
        "w_ple_gate_up": nrm(ks[28], (DEPTH, GATE_RANK, D_MODEL), GATE_RANK ** -0.5),
        "norm_final": gain(ks[29], (D_MODEL,)),
    }


def reference(x, p, rel_bias, norm_mix, w_in, diff_lambda, diff_subln,
              s5_lambda_re, s5_lambda_im, s5_log_dt, s5_b_re, s5_b_im, s5_c_re, s5_c_im,
              s5_d, s5_w_glu, w_gate_up, w_br_diff, w_br_ret, w_br_s5, w_o,
              norm_ffn, w_ffn_gate, w_ffn_up, w_ffn_down,
              norm_ple, w_ple, w_ple_gate_down, w_ple_gate_up, norm_final):
    B, S = x.shape[0], x.shape[1]
    h = x
    for i in range(DEPTH):
        lam_init = 0.8 - 0.6 * math.exp(-0.3 * i)
        xn = rms_norm(h, norm_mix[i])
        z = xn @ w_in[i]
        dq = z[..., OFF_DQ:OFF_DK].reshape(B, S, DIFF_HEADS, 2, DIFF_HEAD_DIM)
        dk = z[..., OFF_DK:OFF_DV].reshape(B, S, DIFF_HEADS, 2, DIFF_HEAD_DIM)
        dv = z[..., OFF_DV:OFF_RQ].reshape(B, S, DIFF_HEADS, 2 * DIFF_HEAD_DIM)
        lq1, lk1, lq2, lk2 = (diff_lambda[i, j].astype(jnp.float32) for j in range(4))
        lam = jnp.exp(jnp.sum(lq1 * lk1)) - jnp.exp(jnp.sum(lq2 * lk2)) + lam_init
        o_diff = diff_attention(dq, dk, dv, rel_bias, lam, diff_subln[i], lam_init)
        rq = z[..., OFF_RQ:OFF_RK].reshape(B, S, RET_HEADS, RET_HEAD_DIM)
        rk = z[..., OFF_RK:OFF_RV].reshape(B, S, RET_HEADS, RET_HEAD_DIM)
        rv = z[..., OFF_RV:OFF_RG].reshape(B, S, RET_HEADS, RET_HEAD_DIM)
        o_ret = retention(rq, rk, rv, z[..., OFF_RG:OFF_SU])
        o_s5 = s5_layer(z[..., OFF_SU:OFF_GATE], s5_lambda_re[i], s5_lambda_im[i], s5_log_dt[i],
                        s5_b_re[i], s5_b_im[i], s5_c_re[i], s5_c_im[i], s5_d[i], s5_w_glu[i])
        gl = z[..., OFF_GATE:]
        wg = w_gate_up[i]
        mixed = (jax.nn.sigmoid(gl @ wg[:, 0:D_MODEL]) * (o_diff @ w_br_diff[i])
                 + jax.nn.sigmoid(gl @ wg[:, D_MODEL:2 * D_MODEL]) * (o_ret @ w_br_ret[i])
                 + jax.nn.sigmoid(gl @ wg[:, 2 * D_MODEL:]) * (o_s5 @ w_br_s5[i]))
        h = h + mixed @ w_o[i]
        hn = rms_norm(h, norm_ffn[i])
        h = h + (jax.nn.silu(hn @ w_ffn_gate[i]) * (hn @ w_ffn_up[i])) @ w_ffn_down[i]
        hp = rms_norm(h, norm_ple[i])
        gate = jax.nn.sigmoid((hp @ w_ple_gate_down[i]) @ w_ple_gate_up[i])
        h = h + (p[i] @ w_ple[i]) * gate
    return rms_norm(h, norm_final)
```

```python
import functools
import math

import jax
import jax.numpy as jnp
from jax import lax
from jax.experimental import pallas as pl
from jax.experimental.pallas import tpu as pltpu

F32 = jnp.float32
BF16 = jnp.bfloat16
EPS = 1e-6

HEAD_DIM = 128
CHUNK = 128
REL_BUCKETS = 32
REL_MAX_DIST = 128
S5_GROUP = 16
S5_STATE = 64
S5_SLICE_GROUPS = 8
MASK_VALUE = -1e30

V7X_VMEM_BYTES = 64 * 1024 * 1024
V7X_VMEM_REQUEST_CAP = 60 * 1024 * 1024
SPILL_AND_TEMP_ALLOWANCE = 8 * 1024 * 1024


def _cparams(n_axes, est_bytes):
    limit = int(min(est_bytes + SPILL_AND_TEMP_ALLOWANCE, V7X_VMEM_REQUEST_CAP))
    return pltpu.CompilerParams(dimension_semantics=("arbitrary",) * n_axes, vmem_limit_bytes=limit)


def _tile(n, want):
    t = min(n, want)
    while n % t:
        t //= 2
    return t


def _rms_rows(x_ref, g_ref, dst_ref, rows):
    n = x_ref.shape[0] // rows

    def body(i, c):
        r = pl.multiple_of(i * rows, rows)
        x = x_ref[pl.ds(r, rows), :]
        ms = jnp.mean(x * x, axis=-1, keepdims=True)
        dst_ref[pl.ds(r, rows), :] = (x * lax.rsqrt(ms + EPS) * g_ref[...]).astype(dst_ref.dtype)
        return c

    lax.fori_loop(0, n, body, 0)


def _norm_mm_kernel(x_ref, g_ref, w_ref, o_ref, xn_ref):
    @pl.when(pl.program_id(1) == 0)
    def _():
        _rms_rows(x_ref, g_ref, xn_ref, min(64, x_ref.shape[0]))

    o_ref[...] = jnp.dot(xn_ref[...], w_ref[...], preferred_element_type=F32).astype(o_ref.dtype)


def _norm_mm(h, gain, w, *, bm, bn, out_dtype=BF16):
    M, K = h.shape
    N = w.shape[1]
    est = 2 * (bm * K * 4 + K * bn * 2 + bm * bn * 4) + bm * K * 2 + bm * bn * 4
    return pl.pallas_call(
        _norm_mm_kernel,
        out_shape=jax.ShapeDtypeStruct((M, N), out_dtype),
        grid=(M // bm, N // bn),
        in_specs=[pl.BlockSpec((bm, K), lambda i, j: (i, 0)),
                  pl.BlockSpec((1, K), lambda i, j: (0, 0)),
                  pl.BlockSpec((K, bn), lambda i, j: (0, j))],
        out_specs=pl.BlockSpec((bm, bn), lambda i, j: (i, j)),
        scratch_shapes=[pltpu.VMEM((bm, K), BF16)],
        compiler_params=_cparams(2, est),
        name="norm_mm",
    )(h, gain.reshape(1, K), w)


def _bias_tile_kernel(rb_ref, o_ref):
    h = pl.program_id(0)
    d = pl.program_id(1)
    T = o_ref.shape[2]
    r = lax.broadcasted_iota(jnp.int32, (T, T), 0)
    c = lax.broadcasted_iota(jnp.int32, (T, T), 1)
    rel = d * T + r - c
    n = jnp.maximum(rel, 0)
    max_exact = REL_BUCKETS // 2
    nf = jnp.maximum(n, 1).astype(F32)
    large = max_exact + (jnp.log(nf / max_exact) / math.log(REL_MAX_DIST / max_exact)
                         * (REL_BUCKETS - max_exact)).astype(jnp.int32)
    large = jnp.minimum(large, REL_BUCKETS - 1)
    bucket = jnp.where(n < max_exact, n, large)
    val = jnp.zeros((T, T), F32)
    for b in range(REL_BUCKETS):
        val = jnp.where(bucket == b, rb_ref[b, h], val)
    o_ref[0, 0] = jnp.where(rel >= 0, val, MASK_VALUE)


def _bias_tiles(rel_bias, n_heads, T):
    return pl.pallas_call(
        _bias_tile_kernel,
        out_shape=jax.ShapeDtypeStruct((n_heads, 2, T, T), F32),
        grid=(n_heads, 2),
        in_specs=[pl.BlockSpec(memory_space=pltpu.SMEM)],
        out_specs=pl.BlockSpec((1, 1, T, T), lambda h, d: (h, d, 0, 0)),
        compiler_params=_cparams(2, 12 * T * T * 4),
        name="t5_bias_tiles",
    )(rel_bias)


def _attn_kernel(qt_ref, kt_ref, rb_ref, li_ref, dl_ref, sub_ref, q_ref, k_ref, v_ref, bt_ref,
                 o_ref, m_ref, l_ref, acc_ref):
    h = pl.program_id(0)
    p = pl.program_id(1)
    qi = qt_ref[p]
    ki = kt_ref[p]
    scale = HEAD_DIM ** -0.5

    @pl.when(ki == 0)
    def _():
        m_ref[...] = jnp.full(m_ref.shape, MASK_VALUE, F32)
        l_ref[...] = jnp.zeros(l_ref.shape, F32)
        acc_ref[...] = jnp.zeros(acc_ref.shape, F32)

    def step(bias):
        v = v_ref[...]
        for c in range(2):
            qc = q_ref[:, c * HEAD_DIM:(c + 1) * HEAD_DIM]
            kc = k_ref[:, c * HEAD_DIM:(c + 1) * HEAD_DIM]
            s = lax.dot_general(qc, kc, (((1,), (1,)), ((), ())), preferred_element_type=F32)
            s = s * scale + bias
            m_prev = m_ref[c]
            m_new = jnp.maximum(m_prev, jnp.max(s, axis=-1, keepdims=True))
            alpha = jnp.exp(m_prev - m_new)
            pr = jnp.exp(s - m_new)
            l_ref[c] = alpha * l_ref[c] + jnp.sum(pr, axis=-1, keepdims=True)
            acc_ref[c] = alpha * acc_ref[c] + jnp.dot(pr.astype(BF16), v, preferred_element_type=F32)
            m_ref[c] = m_new

    @pl.when(qi - ki >= 2)
    def _():
        step(rb_ref[REL_BUCKETS - 1, h])

    @pl.when(qi - ki == 1)
    def _():
        step(bt_ref[0, 1])

    @pl.when(qi == ki)
    def _():
        step(bt_ref[0, 0])
        lam_init = li_ref[0]
        dl = dl_ref[...]
        lam = (jnp.exp(jnp.sum(dl[0:1] * dl[1:2], axis=-1, keepdims=True))
               - jnp.exp(jnp.sum(dl[2:3] * dl[3:4], axis=-1, keepdims=True)) + lam_init)
        o = acc_ref[0] / l_ref[0] - lam * (acc_ref[1] / l_ref[1])
        ms = jnp.mean(o * o, axis=-1, keepdims=True)
        y = (o * lax.rsqrt(ms + EPS) * sub_ref[...]) * (1.0 - lam_init)
        o_ref[...] = y.astype(o_ref.dtype)


def _diff_attention(z, rel_bias, bias_tiles, diff_lambda, subln, lam_init, *, off_q, off_k, off_v,
                    n_heads, T):
    S = z.shape[0]
    W = 2 * HEAD_DIM
    nq = S // T
    qs, ks = [], []
    for qi in range(nq):
        for ki in range(qi + 1):
            qs.append(qi)
            ks.append(ki)
    qt = jnp.asarray(qs, jnp.int32)
    kt = jnp.asarray(ks, jnp.int32)
    bq, bk, bv = off_q // W, off_k // W, off_v // W
    grid_spec = pltpu.PrefetchScalarGridSpec(
        num_scalar_prefetch=2,
        grid=(n_heads, len(qs)),
        in_specs=[pl.BlockSpec(memory_space=pltpu.SMEM),
                  pl.BlockSpec(memory_space=pltpu.SMEM),
                  pl.BlockSpec((4, HEAD_DIM), lambda h, p, qt, kt: (0, 0)),
                  pl.BlockSpec((1, W), lambda h, p, qt, kt: (0, 0)),
                  pl.BlockSpec((T, W), lambda h, p, qt, kt: (qt[p], bq + h)),
                  pl.BlockSpec((T, W), lambda h, p, qt, kt: (kt[p], bk + h)),
                  pl.BlockSpec((T, W), lambda h, p, qt, kt: (kt[p], bv + h)),
                  pl.BlockSpec((1, 2, T, T), lambda h, p, qt, kt: (h, 0, 0, 0))],
        out_specs=pl.BlockSpec((T, W), lambda h, p, qt, kt: (qt[p], h)),
        scratch_shapes=[pltpu.VMEM((2, T, 1), F32), pltpu.VMEM((2, T, 1), F32),
                        pltpu.VMEM((2, T, W), F32)],
    )
    est = 2 * (4 * T * W * 2 + 2 * T * T * 4) + 2 * T * W * 4 + 4 * T * 128 * 4 + 6 * T * T * 4
    return pl.pallas_call(
        _attn_kernel,
        out_shape=jax.ShapeDtypeStruct((S, n_heads * W), BF16),
        grid_spec=grid_spec,
        compiler_params=_cparams(2, est),
        name="diff_attention",
    )(qt, kt, rel_bias, jnp.full((1,), lam_init, F32), diff_lambda, subln.reshape(1, W),
      z, z, z, bias_tiles)


def _ret_kernel(cd_ref, q_ref, k_ref, v_ref, g_ref, cos_ref, sin_ref, dm_ref, xi_ref, ze_ref,
                o_ref, R_ref):
    h = pl.program_id(0)
    t = pl.program_id(1)

    @pl.when(t == 0)
    def _():
        R_ref[...] = jnp.zeros(R_ref.shape, F32)

    dm = dm_ref[0]
    xi = xi_ref[0]
    ze = ze_ref[0]
    cd = cd_ref[h]
    R = R_ref[...]
    half = HEAD_DIM // 2
    for c in range(q_ref.shape[0] // CHUNK):
        sl = slice(c * CHUNK, (c + 1) * CHUNK)
        q = q_ref[sl, :].astype(F32)
        k = k_ref[sl, :].astype(F32)
        v = v_ref[sl, :]
        cs = cos_ref[sl, :]
        sn = sin_ref[sl, :]
        qr = q * cs + pltpu.roll(q, half, 1) * sn
        kr = (k * cs + pltpu.roll(k, half, 1) * sn) * (HEAD_DIM ** -0.5)
        qb = qr.astype(BF16)
        inner = lax.dot_general(qb, kr.astype(BF16), (((1,), (1,)), ((), ())),
                                preferred_element_type=F32) * dm
        o = (jnp.dot(inner.astype(BF16), v, preferred_element_type=F32)
             + jnp.dot(qb, R.astype(BF16), preferred_element_type=F32) * xi)
        R = R * cd + lax.dot_general((kr * ze).astype(BF16), v, (((0,), (0,)), ((), ())),
                                     preferred_element_type=F32)
        o = o * lax.rsqrt(jnp.mean(o * o, axis=-1, keepdims=True) + EPS)
        g = g_ref[sl, :].astype(F32)
        o_ref[sl, :] = (jax.nn.silu(g) * o).astype(o_ref.dtype)
    R_ref[...] = R


def _retention_tables(S, n_heads):
    half = HEAD_DIM // 2
    pos = jnp.arange(S, dtype=F32)
    theta = 1.0 / (10000.0 ** jnp.linspace(0.0, 1.0, half, dtype=F32))
    ang = pos[:, None] * theta[None, :]
    cos = jnp.cos(ang)
    sin = jnp.sin(ang)
    cosf = jnp.concatenate([cos, cos], axis=-1)
    sinf = jnp.concatenate([-sin, sin], axis=-1)
    log_gamma = jnp.log1p(-jnp.exp2(-5.0 - jnp.arange(n_heads, dtype=F32)))
    idx = jnp.arange(CHUNK)
    diff = idx[:, None] - idx[None, :]
    dmat = jnp.where(diff >= 0,
                     jnp.exp(log_gamma[:, None, None] * jnp.maximum(diff, 0).astype(F32)), 0.0)
    xi = jnp.exp(log_gamma[:, None] * (idx[None, :] + 1).astype(F32))[..., None]
    zeta = jnp.exp(log_gamma[:, None] * (CHUNK - 1 - idx)[None, :].astype(F32))[..., None]
    chunk_decay = jnp.exp(log_gamma * CHUNK)
    return cosf, sinf, dmat, xi, zeta, chunk_decay


def _retention(z, tables, *, off_q, off_k, off_v, off_g, n_heads, TR):
    S = z.shape[0]
    cosf, sinf, dmat, xi, zeta, chunk_decay = tables
    bq, bk, bv, bg = (o // HEAD_DIM for o in (off_q, off_k, off_v, off_g))
    blk = lambda b: pl.BlockSpec((TR, HEAD_DIM), lambda h, t: (t, b + h))
    tab = pl.BlockSpec((TR, HEAD_DIM), lambda h, t: (t, 0))
    est = 2 * (5 * TR * HEAD_DIM * 2 + 2 * TR * HEAD_DIM * 4 + 3 * CHUNK * CHUNK * 4) + 64 * CHUNK * CHUNK * 4
    return pl.pallas_call(
        _ret_kernel,
        out_shape=jax.ShapeDtypeStruct((S, n_heads * HEAD_DIM), BF16),
        grid=(n_heads, S // TR),
        in_specs=[pl.BlockSpec(memory_space=pltpu.SMEM),
                  blk(bq), blk(bk), blk(bv), blk(bg), tab, tab,
                  pl.BlockSpec((1, CHUNK, CHUNK), lambda h, t: (h, 0, 0)),
                  pl.BlockSpec((1, CHUNK, 1), lambda h, t: (h, 0, 0)),
                  pl.BlockSpec((1, CHUNK, 1), lambda h, t: (h, 0, 0))],
        out_specs=pl.BlockSpec((TR, HEAD_DIM), lambda h, t: (t, h)),
        scratch_shapes=[pltpu.VMEM((HEAD_DIM, HEAD_DIM), F32)],
        compiler_params=_cparams(2, est),
        name="retention",
    )(chunk_decay, z, z, z, z, cosf, sinf, dmat, xi, zeta)


def _s5_kernel(u_ref, benc_ref, apr_ref, api_ref, cdec_ref, d_ref, wglu_ref, o_ref,
               carry_ref, z_ref, *, nlev):
    t = pl.program_id(0)
    j = pl.program_id(1)
    nsl = pl.num_programs(1)
    Tt = u_ref.shape[0]
    C = apr_ref.shape[2]

    @pl.when((t == 0) & (j == 0))
    def _():
        carry_ref[...] = jnp.zeros(carry_ref.shape, F32)

    u = u_ref[...]
    x = jnp.dot(u, benc_ref[0], preferred_element_type=F32)
    hr = x[:, :C]
    hi = x[:, C:]
    apr = apr_ref[0]
    api = api_ref[0]
    cr = carry_ref[j]
    c_r = cr[:, :C]
    c_i = cr[:, C:]
    a_r = apr[0:1, :]
    a_i = api[0:1, :]
    row = lax.broadcasted_iota(jnp.int32, (Tt, C), 0)
    first = row == 0
    hr = jnp.where(first, hr + (a_r * c_r - a_i * c_i), hr)
    hi = jnp.where(first, hi + (a_r * c_i + a_i * c_r), hi)
    d = 1
    for l in range(nlev):
        a_r = apr[l:l + 1, :]
        a_i = api[l:l + 1, :]
        sr = pltpu.roll(hr, d, 0)
        si = pltpu.roll(hi, d, 0)
        m = row >= d
        nr = hr + jnp.where(m, a_r * sr - a_i * si, 0.0)
        ni = hi + jnp.where(m, a_r * si + a_i * sr, 0.0)
        hr, hi = nr, ni
        d *= 2
    carry_ref[j] = jnp.concatenate([hr[Tt - 1:Tt, :], hi[Tt - 1:Tt, :]], axis=1)
    hcat = jnp.concatenate([hr, hi], axis=1).astype(BF16)
    y = jnp.dot(hcat, cdec_ref[0], preferred_element_type=F32) + d_ref[0] * u.astype(F32)
    z_ref[j] = jax.nn.gelu(y).astype(BF16)

    @pl.when(j == nsl - 1)
    def _():
        zc = jnp.concatenate([z_ref[s] for s in range(z_ref.shape[0])], axis=1)
        zg = jnp.dot(zc, wglu_ref[...], preferred_element_type=F32)
        W = zg.shape[1] // 2
        o_ref[...] = (zg[:, :W] * jax.nn.sigmoid(zg[:, W:])).astype(o_ref.dtype)


def _s5_params(lam_re, lam_im, log_dt, b_re, b_im, c_re, c_im, nlev):
    G, N = lam_re.shape
    sg = S5_SLICE_GROUPS
    nsl = G // sg
    lr, li = lam_re.astype(F32), lam_im.astype(F32)
    dt = jnp.exp(log_dt.astype(F32))[:, None]
    mag = jnp.exp(lr * dt)
    ar = mag * jnp.cos(li * dt)
    ai = mag * jnp.sin(li * dt)
    den = lr * lr + li * li
    fr = ((ar - 1.0) * lr + ai * li) / den
    fi = (ai * lr - (ar - 1.0) * li) / den
    br_, bi_ = b_re.astype(F32), b_im.astype(F32)
    bbr = fr[..., None] * br_ - fi[..., None] * bi_
    bbi = fr[..., None] * bi_ + fi[..., None] * br_
    eye = jnp.eye(sg, dtype=F32)

    def enc(bb):
        return jnp.einsum('sgnj,gh->sgjhn', bb.reshape(nsl, sg, N, S5_GROUP), eye).reshape(
            nsl, sg * S5_GROUP, sg * N)

    def dec(cc):
        return jnp.einsum('sgjn,gh->sgnhj', cc.reshape(nsl, sg, S5_GROUP, N), eye).reshape(
            nsl, sg * N, sg * S5_GROUP)

    benc = jnp.concatenate([enc(bbr), enc(bbi)], axis=-1).astype(BF16)
    cdec = jnp.concatenate([dec(c_re.astype(F32)), -dec(c_im.astype(F32))], axis=1).astype(BF16)
    prs, pis = [], []
    pr, pi = ar, ai
    for _ in range(nlev):
        prs.append(pr)
        pis.append(pi)
        pr, pi = pr * pr - pi * pi, 2.0 * pr * pi
    lp = -(-nlev // 8) * 8
    pad = [jnp.zeros_like(ar)] * (lp - nlev)
    apr = jnp.stack(prs + pad).reshape(lp, nsl, sg * N).transpose(1, 0, 2)
    api = jnp.stack(pis + pad).reshape(lp, nsl, sg * N).transpose(1, 0, 2)
    return benc, cdec, apr, api


def _s5(z, s5p, d_skip, wglu, *, off_u, width, Tt):
    S = z.shape[0]
    benc, cdec, apr, api = s5p
    nsl, uw, xw = benc.shape
    C = xw // 2
    lp = apr.shape[1]
    nlev = int(math.log2(Tt))
    bu = off_u // uw
    est = (2 * (Tt * uw * 2 + uw * xw * 2 + 2 * lp * C * 4 + xw * uw * 2 + width * 2 * width * 2
                + Tt * width * 2) + nsl * Tt * uw * 2 + 24 * Tt * C * 4 + Tt * 2 * width * 4 * 2)
    return pl.pallas_call(
        functools.partial(_s5_kernel, nlev=nlev),
        out_shape=jax.ShapeDtypeStruct((S, width), BF16),
        grid=(S // Tt, nsl),
        in_specs=[pl.BlockSpec((Tt, uw), lambda t, j: (t, bu + j)),
                  pl.BlockSpec((1, uw, xw), lambda t, j: (j, 0, 0)),
                  pl.BlockSpec((1, lp, C), lambda t, j: (j, 0, 0)),
                  pl.BlockSpec((1, lp, C), lambda t, j: (j, 0, 0)),
                  pl.BlockSpec((1, xw, uw), lambda t, j: (j, 0, 0)),
                  pl.BlockSpec((1, 1, uw), lambda t, j: (j, 0, 0)),
                  pl.BlockSpec((width, 2 * width), lambda t, j: (0, 0))],
        out_specs=pl.BlockSpec((Tt, width), lambda t, j: (t, 0)),
        scratch_shapes=[pltpu.VMEM((nsl, 1, xw), F32), pltpu.VMEM((nsl, Tt, uw), BF16)],
        compiler_params=_cparams(2, est),
        name="s5",
    )(z, benc, apr, api, cdec, d_skip.astype(F32).reshape(nsl, 1, uw), wglu)


def _merge_kernel(gl_ref, od_ref, or_ref, os_ref, wg0_ref, wg1_ref, wg2_ref, wd_ref, wr_ref, ws_ref,
                  o_ref):
    gl = gl_ref[...]

    def branch(wg_ref, x_ref, w_ref):
        gate = jax.nn.sigmoid(jnp.dot(gl, wg_ref[...], preferred_element_type=F32))
        return gate * jnp.dot(x_ref[...], w_ref[...], preferred_element_type=F32)

    o_ref[...] = (branch(wg0_ref, od_ref, wd_ref) + branch(wg1_ref, or_ref, wr_ref)
                  + branch(wg2_ref, os_ref, ws_ref)).astype(o_ref.dtype)


def _merge(z, o_diff, o_ret, o_s5, wg, wd, wr, ws, *, off_gate, bm, bn):
    S = z.shape[0]
    R = wg.shape[0]
    D = wd.shape[1]
    nb = D // bn
    kd, kr, ks = wd.shape[0], wr.shape[0], ws.shape[0]
    row = lambda k: pl.BlockSpec((bm, k), lambda i, j: (i, 0))
    col = lambda k: pl.BlockSpec((k, bn), lambda i, j: (0, j))
    est = 2 * (bm * (R + kd + kr + ks) * 2 + (3 * R + kd + kr + ks) * bn * 2 + bm * bn * 2) + 8 * bm * bn * 4
    return pl.pallas_call(
        _merge_kernel,
        out_shape=jax.ShapeDtypeStruct((S, D), BF16),
        grid=(S // bm, nb),
        in_specs=[pl.BlockSpec((bm, R), lambda i, j: (i, off_gate // R)),
                  row(kd), row(kr), row(ks),
                  pl.BlockSpec((R, bn), lambda i, j: (0, j)),
                  pl.BlockSpec((R, bn), lambda i, j: (0, nb + j)),
                  pl.BlockSpec((R, bn), lambda i, j: (0, 2 * nb + j)),
                  col(kd), col(kr), col(ks)],
        out_specs=pl.BlockSpec((bm, bn), lambda i, j: (i, j)),
        compiler_params=_cparams(2, est),
        name="gated_merge",
    )(z, o_diff, o_ret, o_s5, wg, wg, wg, wd, wr, ws)


def _mm_res_kernel(x_ref, w_ref, h_ref, o_ref):
    o_ref[...] = h_ref[...] + jnp.dot(x_ref[...], w_ref[...], preferred_element_type=F32)


def _mm_res(x, w, h, *, bm, bn):
    M, K = x.shape
    N = w.shape[1]
    est = 2 * (bm * K * 2 + K * bn * 2 + 2 * bm * bn * 4) + bm * bn * 4
    return pl.pallas_call(
        _mm_res_kernel,
        out_shape=jax.ShapeDtypeStruct((M, N), F32),
        grid=(M // bm, N // bn),
        in_specs=[pl.BlockSpec((bm, K), lambda i, j: (i, 0)),
                  pl.BlockSpec((K, bn), lambda i, j: (0, j)),
                  pl.BlockSpec((bm, bn), lambda i, j: (i, j))],
        out_specs=pl.BlockSpec((bm, bn), lambda i, j: (i, j)),
        input_output_aliases={2: 0},
        compiler_params=_cparams(2, est),
        name="mm_residual",
    )(x, w, h)


def _ffn_up_kernel(x_ref, g_ref, wg_ref, wu_ref, o_ref, xn_ref):
    @pl.when(pl.program_id(1) == 0)
    def _():
        _rms_rows(x_ref, g_ref, xn_ref, min(64, x_ref.shape[0]))

    xn = xn_ref[...]
    a = jnp.dot(xn, wg_ref[...], preferred_element_type=F32)
    b = jnp.dot(xn, wu_ref[...], preferred_element_type=F32)
    o_ref[...] = (jax.nn.silu(a) * b).astype(o_ref.dtype)


def _ffn_up(h, gain, wg, wu, *, bm, bn):
    M, K = h.shape
    N = wg.shape[1]
    est = 2 * (bm * K * 4 + 2 * K * bn * 2 + bm * bn * 2) + bm * K * 2 + 4 * bm * bn * 4
    return pl.pallas_call(
        _ffn_up_kernel,
        out_shape=jax.ShapeDtypeStruct((M, N), BF16),
        grid=(M // bm, N // bn),
        in_specs=[pl.BlockSpec((bm, K), lambda i, j: (i, 0)),
                  pl.BlockSpec((1, K), lambda i, j: (0, 0)),
                  pl.BlockSpec((K, bn), lambda i, j: (0, j)),
                  pl.BlockSpec((K, bn), lambda i, j: (0, j))],
        out_specs=pl.BlockSpec((bm, bn), lambda i, j: (i, j)),
        scratch_shapes=[pltpu.VMEM((bm, K), BF16)],
        compiler_params=_cparams(2, est),
        name="ffn_up",
    )(h, gain.reshape(1, K), wg, wu)


def _ple_kernel(h_ref, g_ref, p_ref, wple_ref, wgd_ref, wgu_ref, o_ref, hp_ref):
    _rms_rows(h_ref, g_ref, hp_ref, min(64, h_ref.shape[0]))
    t = jnp.dot(hp_ref[...], wgd_ref[...], preferred_element_type=F32).astype(BF16)
    gate = jax.nn.sigmoid(jnp.dot(t, wgu_ref[...], preferred_element_type=F32))
    e = jnp.dot(p_ref[...].astype(BF16), wple_ref[...], preferred_element_type=F32)
    o_ref[...] = h_ref[...] + e * gate


def _ple(h, gain, p, wple, wgd, wgu, *, bm):
    M, D = h.shape
    P = p.shape[1]
    R = wgd.shape[1]
    est = 2 * (2 * bm * D * 4 + bm * P * 4 + (P + 2 * R) * D * 2) + bm * D * 2 + 4 * bm * D * 4
    return pl.pallas_call(
        _ple_kernel,
        out_shape=jax.ShapeDtypeStruct((M, D), F32),
        grid=(M // bm,),
        in_specs=[pl.BlockSpec((bm, D), lambda i: (i, 0)),
                  pl.BlockSpec((1, D), lambda i: (0, 0)),
                  pl.BlockSpec((bm, P), lambda i: (i, 0)),
                  pl.BlockSpec((P, D), lambda i: (0, 0)),
                  pl.BlockSpec((D, R), lambda i: (0, 0)),
                  pl.BlockSpec((R, D), lambda i: (0, 0))],
        out_specs=pl.BlockSpec((bm, D), lambda i: (i, 0)),
        scratch_shapes=[pltpu.VMEM((bm, D), BF16)],
        input_output_aliases={0: 0},
        compiler_params=_cparams(1, est),
        name="ple",
    )(h, gain.reshape(1, D), p, wple, wgd, wgu)


def _final_norm_kernel(x_ref, g_ref, o_ref):
    _rms_rows(x_ref, g_ref, o_ref, min(64, x_ref.shape[0]))


def _final_norm(h, gain, *, bm):
    M, D = h.shape
    return pl.pallas_call(
        _final_norm_kernel,
        out_shape=jax.ShapeDtypeStruct((M, D), F32),
        grid=(M // bm,),
        in_specs=[pl.BlockSpec((bm, D), lambda i: (i, 0)), pl.BlockSpec((1, D), lambda i: (0, 0))],
        out_specs=pl.BlockSpec((bm, D), lambda i: (i, 0)),
        compiler_params=_cparams(1, 4 * bm * D * 4 + 64 * D * 4 * 4),
        name="final_norm",
    )(h, gain.reshape(1, D))


def kernel(x, p, rel_bias, norm_mix, w_in, diff_lambda, diff_subln, s5_lambda_re, s5_lambda_im, s5_log_dt, s5_b_re, s5_b_im, s5_c_re, s5_c_im, s5_d, s5_w_glu, w_gate_up, w_br_diff, w_br_ret, w_br_s5, w_o, norm_ffn, w_ffn_gate, w_ffn_up, w_ffn_down, norm_ple, w_ple, w_ple_gate_down, w_ple_gate_up, norm_final):
    B, S, D = x.shape
    assert B == 1
    depth = w_in.shape[0]
    diff_w, ret_w, s5_w = w_br_diff.shape[1], w_br_ret.shape[1], w_br_s5.shape[1]
    gate_rank = w_gate_up.shape[1]
    diff_heads = diff_w // (2 * HEAD_DIM)
    ret_heads = ret_w // HEAD_DIM
    off_dq = 0
    off_dk = off_dq + diff_w
    off_dv = off_dk + diff_w
    off_rq = off_dv + diff_w
    off_rk = off_rq + ret_w
    off_rv = off_rk + ret_w
    off_rg = off_rv + ret_w
    off_su = off_rg + ret_w
    off_gate = off_su + s5_w
    in_w = off_gate + gate_rank
    assert w_in.shape[2] == in_w
    hidden = w_ffn_gate.shape[2]
    hidden_pad = -(-hidden // 512) * 512

    T_attn = _tile(S, 512)
    T_ret = _tile(S, 512)
    T_s5 = _tile(S, 256)
    bm_in = _tile(S, 512)
    bn_in = _tile(in_w, 768) if in_w % 768 == 0 else _tile(in_w, 512)
    bm_mg, bn_mg = _tile(S, 1024), _tile(D, 512)
    bm_o, bn_o = _tile(S, 1024), _tile(D, 512)
    bm_up, bn_up = _tile(S, 512), _tile(hidden_pad, 512)
    bm_dn, bn_dn = _tile(S, 512), _tile(D, 256)
    bm_ple = _tile(S, 256)

    h = x.reshape(S, D)
    bias_tiles = _bias_tiles(rel_bias, diff_heads, T_attn)
    ret_tables = _retention_tables(S, ret_heads)
    nlev = int(math.log2(T_s5))

    for i in range(depth):
        lam_init = 0.8 - 0.6 * math.exp(-0.3 * i)
        z = _norm_mm(h, norm_mix[i], w_in[i].astype(BF16), bm=bm_in, bn=bn_in)
        o_diff = _diff_attention(z, rel_bias, bias_tiles, diff_lambda[i], diff_subln[i], lam_init,
                                 off_q=off_dq, off_k=off_dk, off_v=off_dv, n_heads=diff_heads, T=T_attn)
        o_ret = _retention(z, ret_tables, off_q=off_rq, off_k=off_rk, off_v=off_rv, off_g=off_rg,
                           n_heads=ret_heads, TR=T_ret)
        s5p = _s5_params(s5_lambda_re[i], s5_lambda_im[i], s5_log_dt[i], s5_b_re[i], s5_b_im[i],
                         s5_c_re[i], s5_c_im[i], nlev)
        o_s5 = _s5(z, s5p, s5_d[i], s5_w_glu[i].astype(BF16), off_u=off_su, width=s5_w, Tt=T_s5)
        mixed = _merge(z, o_diff, o_ret, o_s5, w_gate_up[i].astype(BF16), w_br_diff[i].astype(BF16),
                       w_br_ret[i].astype(BF16), w_br_s5[i].astype(BF16), off_gate=off_gate,
                       bm=bm_mg, bn=bn_mg)
        h = _mm_res(mixed, w_o[i].astype(BF16), h, bm=bm_o, bn=bn_o)
        padc = ((0, 0), (0, hidden_pad - hidden))
        act = _ffn_up(h, norm_ffn[i], jnp.pad(w_ffn_gate[i].astype(BF16), padc),
                      jnp.pad(w_ffn_up[i].astype(BF16), padc), bm=bm_up, bn=bn_up)
        wdn = jnp.pad(w_ffn_down[i].astype(BF16), ((0, hidden_pad - hidden), (0, 0)))
        h = _mm_res(act, wdn, h, bm=bm_dn, bn=bn_dn)
        h = _ple(h, norm_ple[i], p[i].reshape(S, -1), w_ple[i].astype(BF16),
                 w_ple_gate_down[i].astype(BF16), w_ple_gate_up[i].astype(BF16), bm=bm_ple)
    out = _final_norm(h, norm_final, bm=_tile(S, 256))
    return out.reshape(B, S, D)
```

```python
import functools
import math

import jax
import jax.numpy as jnp
from jax import lax
from jax.experimental import pallas as pl
from jax.experimental.pallas import tpu as pltpu

F32 = jnp.float32
BF16 = jnp.bfloat16
EPS = 1e-6

HEAD_DIM = 128
CHUNK = 128
REL_BUCKETS = 32
REL_MAX_DIST = 128
S5_GROUP = 16
S5_STATE = 64
S5_SLICE_GROUPS = 8
MASK_VALUE = -1e30
LOG2E = math.log2(math.e)
ATTN_ROW_CHUNK = 128

V7X_VMEM_BYTES = 64 * 1024 * 1024
V7X_VMEM_REQUEST_CAP = 60 * 1024 * 1024
SPILL_AND_TEMP_ALLOWANCE = 8 * 1024 * 1024


def _cparams(n_axes, est_bytes):
    limit = int(min(est_bytes + SPILL_AND_TEMP_ALLOWANCE, V7X_VMEM_REQUEST_CAP))
    return pltpu.CompilerParams(dimension_semantics=("arbitrary",) * n_axes, vmem_limit_bytes=limit)


def _tile(n, want):
    t = min(n, want)
    while n % t:
        t //= 2
    return t


def _rms_rows(x_ref, g_ref, dst_ref, rows):
    n = x_ref.shape[0] // rows

    def body(i, c):
        r = pl.multiple_of(i * rows, rows)
        x = x_ref[pl.ds(r, rows), :]
        ms = jnp.mean(x * x, axis=-1, keepdims=True)
        dst_ref[pl.ds(r, rows), :] = (x * lax.rsqrt(ms + EPS) * g_ref[...]).astype(dst_ref.dtype)
        return c

    lax.fori_loop(0, n, body, 0)


def _norm_mm_kernel(x_ref, g_ref, w_ref, cs_ref, o_ref, xn_ref):
    @pl.when(pl.program_id(1) == 0)
    def _():
        _rms_rows(x_ref, g_ref, xn_ref, min(64, x_ref.shape[0]))

    acc = jnp.dot(xn_ref[...], w_ref[...], preferred_element_type=F32)
    o_ref[...] = (acc * cs_ref[...]).astype(o_ref.dtype)


def _wspec(layer, k, bn, col_block=lambda j: j):
    return pl.BlockSpec((None, k, bn), lambda i, j: (layer, 0, col_block(j)))


def _norm_mm(h, gain, w, layer, col_scale, *, bm, bn, out_dtype=BF16):
    M, K = h.shape
    N = w.shape[2]
    est = 2 * (bm * K * 4 + K * bn * 2 + bm * bn * 4) + bm * K * 2 + bm * bn * 4
    return pl.pallas_call(
        _norm_mm_kernel,
        out_shape=jax.ShapeDtypeStruct((M, N), out_dtype),
        grid=(M // bm, N // bn),
        in_specs=[pl.BlockSpec((bm, K), lambda i, j: (i, 0)),
                  pl.BlockSpec((1, K), lambda i, j: (0, 0)),
                  _wspec(layer, K, bn),
                  pl.BlockSpec((1, bn), lambda i, j: (0, j))],
        out_specs=pl.BlockSpec((bm, bn), lambda i, j: (i, j)),
        scratch_shapes=[pltpu.VMEM((bm, K), BF16)],
        compiler_params=_cparams(2, est),
        name="norm_mm",
    )(h, gain.reshape(1, K), w, col_scale.reshape(1, N))


def _bias_tile_kernel(rb_ref, o_ref):
    h = pl.program_id(0)
    d = pl.program_id(1)
    T = o_ref.shape[2]
    r = lax.broadcasted_iota(jnp.int32, (T, T), 0)
    c = lax.broadcasted_iota(jnp.int32, (T, T), 1)
    rel = d * T + r - c
    n = jnp.maximum(rel, 0)
    max_exact = REL_BUCKETS // 2
    nf = jnp.maximum(n, 1).astype(F32)
    large = max_exact + (jnp.log(nf / max_exact) / math.log(REL_MAX_DIST / max_exact)
                         * (REL_BUCKETS - max_exact)).astype(jnp.int32)
    large = jnp.minimum(large, REL_BUCKETS - 1)
    bucket = jnp.where(n < max_exact, n, large)
    val = jnp.zeros((T, T), F32)
    for b in range(REL_BUCKETS):
        val = jnp.where(bucket == b, rb_ref[b, h], val)
    o_ref[0, 0] = jnp.where(rel >= 0, val * LOG2E, MASK_VALUE)


def _bias_tiles(rel_bias, n_heads, T):
    return pl.pallas_call(
        _bias_tile_kernel,
        out_shape=jax.ShapeDtypeStruct((n_heads, 2, T, T), F32),
        grid=(n_heads, 2),
        in_specs=[pl.BlockSpec(memory_space=pltpu.SMEM)],
        out_specs=pl.BlockSpec((1, 1, T, T), lambda h, d: (h, d, 0, 0)),
        compiler_params=_cparams(2, 12 * T * T * 4),
        name="t5_bias_tiles",
    )(rel_bias)


def _attn_kernel(qt_ref, kt_ref, rb_ref, li_ref, dl_ref, sub_ref, q_ref, k_ref, v_ref, bt_ref,
                 o_ref, m_ref, l_ref, acc_ref, *, row_chunks):
    h = pl.program_id(0)
    p = pl.program_id(1)
    qi = qt_ref[p]
    ki = kt_ref[p]
    T = q_ref.shape[0]
    W = v_ref.shape[1]

    @pl.when(ki == 0)
    def _():
        m_ref[...] = jnp.full(m_ref.shape, MASK_VALUE, F32)
        l_ref[...] = jnp.zeros(l_ref.shape, F32)
        acc_ref[...] = jnp.zeros(acc_ref.shape, F32)

    def step(bias_tile_ref, bias_const):
        v = v_ref[...]
        R = T // row_chunks
        for r in range(row_chunks):
            rows = slice(r * R, (r + 1) * R)
            for c in range(2):
                qc = q_ref[rows, c * HEAD_DIM:(c + 1) * HEAD_DIM]
                kc = k_ref[:, c * HEAD_DIM:(c + 1) * HEAD_DIM]
                s = lax.dot_general(qc, kc, (((1,), (1,)), ((), ())), preferred_element_type=F32)
                if bias_tile_ref is not None:
                    s = s + bias_tile_ref[rows, :]
                m_prev = m_ref[c, rows, :]
                row_max = jnp.max(s, axis=-1, keepdims=True)
                if bias_const is not None:
                    row_max = row_max + bias_const
                m_new = jnp.maximum(m_prev, row_max)
                alpha = jnp.exp2(m_prev - m_new)
                shift = m_new if bias_const is None else m_new - bias_const
                l_part = alpha * l_ref[c, rows, :]
                ps = []
                for j in range(T // 128):
                    pj = jnp.exp2(s[:, j * 128:(j + 1) * 128] - shift)
                    l_part = l_part + pj
                    ps.append(pj.astype(BF16))
                l_ref[c, rows, :] = l_part
                m_ref[c, rows, :] = m_new
                pv = jnp.dot(jnp.concatenate(ps, axis=1), v, preferred_element_type=F32)
                acc_ref[c, rows, :] = (jnp.concatenate([alpha] * (W // 128), axis=1) * acc_ref[c, rows, :]
                                       + pv)

    @pl.when(qi - ki >= 2)
    def _():
        step(None, rb_ref[REL_BUCKETS - 1, h] * LOG2E)

    @pl.when(qi - ki == 1)
    def _():
        step(bt_ref.at[0, 1], None)

    @pl.when(qi == ki)
    def _():
        step(bt_ref.at[0, 0], None)
        lam_init = li_ref[0]
        dl = dl_ref[...]
        lam = (jnp.exp(jnp.sum(dl[0:1] * dl[1:2], axis=-1, keepdims=True))
               - jnp.exp(jnp.sum(dl[2:3] * dl[3:4], axis=-1, keepdims=True)) + lam_init)
        l0 = jnp.sum(l_ref[0], axis=-1, keepdims=True)
        l1 = jnp.sum(l_ref[1], axis=-1, keepdims=True)
        o = acc_ref[0] / l0 - lam * (acc_ref[1] / l1)
        ms = jnp.mean(o * o, axis=-1, keepdims=True)
        y = (o * lax.rsqrt(ms + EPS) * sub_ref[...]) * (1.0 - lam_init)
        o_ref[...] = y.astype(o_ref.dtype)


def _diff_attention(z, rel_bias, bias_tiles, diff_lambda, subln, lam_init, *, off_q, off_k, off_v,
                    n_heads, T):
    S = z.shape[0]
    W = 2 * HEAD_DIM
    nq = S // T
    qs, ks = [], []
    for qi in range(nq):
        for ki in range(qi + 1):
            qs.append(qi)
            ks.append(ki)
    qt = jnp.asarray(qs, jnp.int32)
    kt = jnp.asarray(ks, jnp.int32)
    bq, bk, bv = off_q // W, off_k // W, off_v // W
    grid_spec = pltpu.PrefetchScalarGridSpec(
        num_scalar_prefetch=2,
        grid=(n_heads, len(qs)),
        in_specs=[pl.BlockSpec(memory_space=pltpu.SMEM),
                  pl.BlockSpec(memory_space=pltpu.SMEM),
                  pl.BlockSpec((4, HEAD_DIM), lambda h, p, qt, kt: (0, 0)),
                  pl.BlockSpec((1, W), lambda h, p, qt, kt: (0, 0)),
                  pl.BlockSpec((T, W), lambda h, p, qt, kt: (qt[p], bq + h)),
                  pl.BlockSpec((T, W), lambda h, p, qt, kt: (kt[p], bk + h)),
                  pl.BlockSpec((T, W), lambda h, p, qt, kt: (kt[p], bv + h)),
                  pl.BlockSpec((1, 2, T, T), lambda h, p, qt, kt: (h, 0, 0, 0))],
        out_specs=pl.BlockSpec((T, W), lambda h, p, qt, kt: (qt[p], h)),
        scratch_shapes=[pltpu.VMEM((2, T, 128), F32), pltpu.VMEM((2, T, 128), F32),
                        pltpu.VMEM((2, T, W), F32)],
    )
    est = 2 * (4 * T * W * 2 + 2 * T * T * 4) + 2 * T * W * 4 + 4 * T * 128 * 4 + 5 * T * T * 4
    return pl.pallas_call(
        functools.partial(_attn_kernel, row_chunks=max(1, T // ATTN_ROW_CHUNK)),
        out_shape=jax.ShapeDtypeStruct((S, n_heads * W), BF16),
        grid_spec=grid_spec,
        compiler_params=_cparams(2, est),
        name="diff_attention",
    )(qt, kt, rel_bias, jnp.full((1,), lam_init, F32), diff_lambda, subln.reshape(1, W),
      z, z, z, bias_tiles)


def _ret_kernel(cd_ref, q_ref, k_ref, v_ref, g_ref, cos_ref, sin_ref, dm_ref, xi_ref, ze_ref,
                o_ref, R_ref):
    h = pl.program_id(0)
    t = pl.program_id(1)

    @pl.when(t == 0)
    def _():
        R_ref[...] = jnp.zeros(R_ref.shape, F32)

    dm = dm_ref[0]
    xi = xi_ref[0]
    ze = ze_ref[0]
    cd = cd_ref[h]
    R = R_ref[...]
    half = HEAD_DIM // 2
    for c in range(q_ref.shape[0] // CHUNK):
        sl = slice(c * CHUNK, (c + 1) * CHUNK)
        q = q_ref[sl, :].astype(F32)
        k = k_ref[sl, :].astype(F32)
        v = v_ref[sl, :]
        cs = cos_ref[sl, :]
        sn = sin_ref[sl, :]
        qr = q * cs + pltpu.roll(q, half, 1) * sn
        kr = (k * cs + pltpu.roll(k, half, 1) * sn) * (HEAD_DIM ** -0.5)
        qb = qr.astype(BF16)
        inner = lax.dot_general(qb, kr.astype(BF16), (((1,), (1,)), ((), ())),
                                preferred_element_type=F32) * dm
        o = (jnp.dot(inner.astype(BF16), v, preferred_element_type=F32)
             + jnp.dot(qb, R.astype(BF16), preferred_element_type=F32) * xi)
        R = R * cd + lax.dot_general((kr * ze).astype(BF16), v, (((0,), (0,)), ((), ())),
                                     preferred_element_type=F32)
        o = o * lax.rsqrt(jnp.mean(o * o, axis=-1, keepdims=True) + EPS)
        g = g_ref[sl, :].astype(F32)
        o_ref[sl, :] = (jax.nn.silu(g) * o).astype(o_ref.dtype)
    R_ref[...] = R


def _retention_tables(S, n_heads):
    half = HEAD_DIM // 2
    pos = jnp.arange(S, dtype=F32)
    theta = 1.0 / (10000.0 ** jnp.linspace(0.0, 1.0, half, dtype=F32))
    ang = pos[:, None] * theta[None, :]
    cos = jnp.cos(ang)
    sin = jnp.sin(ang)
    cosf = jnp.concatenate([cos, cos], axis=-1)
    sinf = jnp.concatenate([-sin, sin], axis=-1)
    log_gamma = jnp.log1p(-jnp.exp2(-5.0 - jnp.arange(n_heads, dtype=F32)))
    idx = jnp.arange(CHUNK)
    diff = idx[:, None] - idx[None, :]
    dmat = jnp.where(diff >= 0,
                     jnp.exp(log_gamma[:, None, None] * jnp.maximum(diff, 0).astype(F32)), 0.0)
    xi = jnp.exp(log_gamma[:, None] * (idx[None, :] + 1).astype(F32))[..., None]
    zeta = jnp.exp(log_gamma[:, None] * (CHUNK - 1 - idx)[None, :].astype(F32))[..., None]
    chunk_decay = jnp.exp(log_gamma * CHUNK)
    return cosf, sinf, dmat, xi, zeta, chunk_decay


def _retention(z, tables, *, off_q, off_k, off_v, off_g, n_heads, TR):
    S = z.shape[0]
    cosf, sinf, dmat, xi, zeta, chunk_decay = tables
    bq, bk, bv, bg = (o // HEAD_DIM for o in (off_q, off_k, off_v, off_g))
    blk = lambda b: pl.BlockSpec((TR, HEAD_DIM), lambda h, t: (t, b + h))
    tab = pl.BlockSpec((TR, HEAD_DIM), lambda h, t: (t, 0))
    est = 2 * (5 * TR * HEAD_DIM * 2 + 2 * TR * HEAD_DIM * 4 + 3 * CHUNK * CHUNK * 4) + 64 * CHUNK * CHUNK * 4
    return pl.pallas_call(
        _ret_kernel,
        out_shape=jax.ShapeDtypeStruct((S, n_heads * HEAD_DIM), BF16),
        grid=(n_heads, S // TR),
        in_specs=[pl.BlockSpec(memory_space=pltpu.SMEM),
                  blk(bq), blk(bk), blk(bv), blk(bg), tab, tab,
                  pl.BlockSpec((1, CHUNK, CHUNK), lambda h, t: (h, 0, 0)),
                  pl.BlockSpec((1, CHUNK, 1), lambda h, t: (h, 0, 0)),
                  pl.BlockSpec((1, CHUNK, 1), lambda h, t: (h, 0, 0))],
        out_specs=pl.BlockSpec((TR, HEAD_DIM), lambda h, t: (t, h)),
        scratch_shapes=[pltpu.VMEM((HEAD_DIM, HEAD_DIM), F32)],
        compiler_params=_cparams(2, est),
        name="retention",
    )(chunk_decay, z, z, z, z, cosf, sinf, dmat, xi, zeta)


def _s5_kernel(u_ref, benc_ref, apr_ref, api_ref, cdec_ref, d_ref, wglu_ref, o_ref,
               carry_ref, z_ref):
    t = pl.program_id(0)
    j = pl.program_id(1)
    nsl = pl.num_programs(1)
    Tt = u_ref.shape[0]
    C = apr_ref.shape[2]

    @pl.when((t == 0) & (j == 0))
    def _():
        carry_ref[...] = jnp.zeros(carry_ref.shape, F32)

    u = u_ref[...]
    x = jnp.dot(u, benc_ref[0], preferred_element_type=F32)
    ng = Tt // 8
    hr = x[:, :C].reshape(ng, 8, C)
    hi = x[:, C:].reshape(ng, 8, C)
    apr = apr_ref[0]
    api = api_ref[0]
    sub = lax.broadcasted_iota(jnp.int32, (ng, 8, C), 1)
    for d in (1, 2, 4):
        a_r = apr[d - 1:d, :].reshape(1, 1, C)
        a_i = api[d - 1:d, :].reshape(1, 1, C)
        sr = pltpu.roll(hr, d, 1)
        si = pltpu.roll(hi, d, 1)
        m = sub >= d
        nr = hr + jnp.where(m, a_r * sr - a_i * si, 0.0)
        ni = hi + jnp.where(m, a_r * si + a_i * sr, 0.0)
        hr, hi = nr, ni
    cr = carry_ref[j]
    c_r = cr[:, :C]
    c_i = cr[:, C:]
    out_r, out_i = [], []
    for g in range(ng):
        b_r = jnp.broadcast_to(c_r, (8, C))
        b_i = jnp.broadcast_to(c_i, (8, C))
        g_r = hr[g] + (apr * b_r - api * b_i)
        g_i = hi[g] + (apr * b_i + api * b_r)
        out_r.append(g_r)
        out_i.append(g_i)
        c_r = g_r[7:8, :]
        c_i = g_i[7:8, :]
    carry_ref[j] = jnp.concatenate([c_r, c_i], axis=1)
    hcat = jnp.concatenate([jnp.concatenate(out_r, axis=0), jnp.concatenate(out_i, axis=0)],
                           axis=1).astype(BF16)
    y = jnp.dot(hcat, cdec_ref[0], preferred_element_type=F32) + d_ref[0] * u.astype(F32)
    z_ref[j] = jax.nn.gelu(y).astype(BF16)

    @pl.when(j == nsl - 1)
    def _():
        zc = jnp.concatenate([z_ref[s] for s in range(z_ref.shape[0])], axis=1)
        zg = jnp.dot(zc, wglu_ref[...], preferred_element_type=F32)
        W = zg.shape[1] // 2
        o_ref[...] = (zg[:, :W] * jax.nn.sigmoid(zg[:, W:])).astype(o_ref.dtype)


def _s5_params(lam_re, lam_im, log_dt, b_re, b_im, c_re, c_im):
    G, N = lam_re.shape
    sg = S5_SLICE_GROUPS
    nsl = G // sg
    lr, li = lam_re.astype(F32), lam_im.astype(F32)
    dt = jnp.exp(log_dt.astype(F32))[:, None]
    mag = jnp.exp(lr * dt)
    ar = mag * jnp.cos(li * dt)
    ai = mag * jnp.sin(li * dt)
    den = lr * lr + li * li
    fr = ((ar - 1.0) * lr + ai * li) / den
    fi = (ai * lr - (ar - 1.0) * li) / den
    br_, bi_ = b_re.astype(F32), b_im.astype(F32)
    bbr = fr[..., None] * br_ - fi[..., None] * bi_
    bbi = fr[..., None] * bi_ + fi[..., None] * br_
    eye = jnp.eye(sg, dtype=F32)

    def enc(bb):
        return jnp.einsum('sgnj,gh->sgjhn', bb.reshape(nsl, sg, N, S5_GROUP), eye).reshape(
            nsl, sg * S5_GROUP, sg * N)

    def dec(cc):
        return jnp.einsum('sgjn,gh->sgnhj', cc.reshape(nsl, sg, S5_GROUP, N), eye).reshape(
            nsl, sg * N, sg * S5_GROUP)

    benc = jnp.concatenate([enc(bbr), enc(bbi)], axis=-1).astype(BF16)
    cdec = jnp.concatenate([dec(c_re.astype(F32)), -dec(c_im.astype(F32))], axis=1).astype(BF16)
    prs, pis = [ar], [ai]
    for _ in range(7):
        pr, pi = prs[-1], pis[-1]
        prs.append(pr * ar - pi * ai)
        pis.append(pr * ai + pi * ar)
    apr = jnp.stack(prs).reshape(8, nsl, sg * N).transpose(1, 0, 2)
    api = jnp.stack(pis).reshape(8, nsl, sg * N).transpose(1, 0, 2)
    return benc, cdec, apr, api


def _s5(z, s5p, d_skip, wglu, layer, *, off_u, width, Tt):
    S = z.shape[0]
    benc, cdec, apr, api = s5p
    nsl, uw, xw = benc.shape
    C = xw // 2
    lp = apr.shape[1]
    bu = off_u // uw
    est = (2 * (Tt * uw * 2 + uw * xw * 2 + 2 * lp * C * 4 + xw * uw * 2 + width * 2 * width * 2
                + Tt * width * 2) + nsl * Tt * uw * 2 + 24 * Tt * C * 4 + Tt * 2 * width * 4 * 2)
    return pl.pallas_call(
        _s5_kernel,
        out_shape=jax.ShapeDtypeStruct((S, width), BF16),
        grid=(S // Tt, nsl),
        in_specs=[pl.BlockSpec((Tt, uw), lambda t, j: (t, bu + j)),
                  pl.BlockSpec((1, uw, xw), lambda t, j: (j, 0, 0)),
                  pl.BlockSpec((1, lp, C), lambda t, j: (j, 0, 0)),
                  pl.BlockSpec((1, lp, C), lambda t, j: (j, 0, 0)),
                  pl.BlockSpec((1, xw, uw), lambda t, j: (j, 0, 0)),
                  pl.BlockSpec((1, 1, uw), lambda t, j: (j, 0, 0)),
                  pl.BlockSpec((None, width, 2 * width), lambda t, j: (layer, 0, 0))],
        out_specs=pl.BlockSpec((Tt, width), lambda t, j: (t, 0)),
        scratch_shapes=[pltpu.VMEM((nsl, 1, xw), F32), pltpu.VMEM((nsl, Tt, uw), BF16)],
        compiler_params=_cparams(2, est),
        name="s5",
    )(z, benc, apr, api, cdec, d_skip.astype(F32).reshape(nsl, 1, uw), wglu)


def _merge_kernel(gl_ref, od_ref, or_ref, os_ref, wg0_ref, wg1_ref, wg2_ref, wd_ref, wr_ref, ws_ref,
                  o_ref):
    gl = gl_ref[...]

    def branch(wg_ref, x_ref, w_ref):
        gate = jax.nn.sigmoid(jnp.dot(gl, wg_ref[...], preferred_element_type=F32))
        return gate * jnp.dot(x_ref[...], w_ref[...], preferred_element_type=F32)

    o_ref[...] = (branch(wg0_ref, od_ref, wd_ref) + branch(wg1_ref, or_ref, wr_ref)
                  + branch(wg2_ref, os_ref, ws_ref)).astype(o_ref.dtype)


def _merge(z, o_diff, o_ret, o_s5, wg, wd, wr, ws, layer, *, off_gate, bm, bn):
    S = z.shape[0]
    R = wg.shape[1]
    D = wd.shape[2]
    nb = D // bn
    kd, kr, ks = wd.shape[1], wr.shape[1], ws.shape[1]
    row = lambda k: pl.BlockSpec((bm, k), lambda i, j: (i, 0))
    col = lambda k: _wspec(layer, k, bn)
    est = 2 * (bm * (R + kd + kr + ks) * 2 + (3 * R + kd + kr + ks) * bn * 2 + bm * bn * 2) + 8 * bm * bn * 4
    return pl.pallas_call(
        _merge_kernel,
        out_shape=jax.ShapeDtypeStruct((S, D), BF16),
        grid=(S // bm, nb),
        in_specs=[pl.BlockSpec((bm, R), lambda i, j: (i, off_gate // R)),
                  row(kd), row(kr), row(ks),
                  _wspec(layer, R, bn),
                  _wspec(layer, R, bn, lambda j: nb + j),
                  _wspec(layer, R, bn, lambda j: 2 * nb + j),
                  col(kd), col(kr), col(ks)],
        out_specs=pl.BlockSpec((bm, bn), lambda i, j: (i, j)),
        compiler_params=_cparams(2, est),
        name="gated_merge",
    )(z, o_diff, o_ret, o_s5, wg, wg, wg, wd, wr, ws)


def _mm_res_kernel(x_ref, w_ref, h_ref, o_ref):
    o_ref[...] = h_ref[...] + jnp.dot(x_ref[...], w_ref[...], preferred_element_type=F32)


def _mm_res(x, w, layer, h, *, bm, bn):
    M, K = x.shape
    N = w.shape[2]
    est = 2 * (bm * K * 2 + K * bn * 2 + 2 * bm * bn * 4) + bm * bn * 4
    return pl.pallas_call(
        _mm_res_kernel,
        out_shape=jax.ShapeDtypeStruct((M, N), F32),
        grid=(M // bm, N // bn),
        in_specs=[pl.BlockSpec((bm, K), lambda i, j: (i, 0)),
                  _wspec(layer, K, bn),
                  pl.BlockSpec((bm, bn), lambda i, j: (i, j))],
        out_specs=pl.BlockSpec((bm, bn), lambda i, j: (i, j)),
        input_output_aliases={2: 0},
        compiler_params=_cparams(2, est),
        name="mm_residual",
    )(x, w, h)


def _ffn_up_kernel(x_ref, g_ref, wg_ref, wu_ref, o_ref, xn_ref):
    @pl.when(pl.program_id(1) == 0)
    def _():
        _rms_rows(x_ref, g_ref, xn_ref, min(64, x_ref.shape[0]))

    xn = xn_ref[...]
    a = jnp.dot(xn, wg_ref[...], preferred_element_type=F32)
    b = jnp.dot(xn, wu_ref[...], preferred_element_type=F32)
    o_ref[...] = (jax.nn.silu(a) * b).astype(o_ref.dtype)


def _ffn_up(h, gain, wg, wu, layer, *, bm, bn):
    M, K = h.shape
    N = wg.shape[2]
    est = 2 * (bm * K * 4 + 2 * K * bn * 2 + bm * bn * 2) + bm * K * 2 + 4 * bm * bn * 4
    return pl.pallas_call(
        _ffn_up_kernel,
        out_shape=jax.ShapeDtypeStruct((M, N), BF16),
        grid=(M // bm, pl.cdiv(N, bn)),
        in_specs=[pl.BlockSpec((bm, K), lambda i, j: (i, 0)),
                  pl.BlockSpec((1, K), lambda i, j: (0, 0)),
                  _wspec(layer, K, bn),
                  _wspec(layer, K, bn)],
        out_specs=pl.BlockSpec((bm, bn), lambda i, j: (i, j)),
        scratch_shapes=[pltpu.VMEM((bm, K), BF16)],
        compiler_params=_cparams(2, est),
        name="ffn_up",
    )(h, gain.reshape(1, K), wg, wu)


def _ple_kernel(h_ref, g_ref, p_ref, wple_ref, wgd_ref, wgu_ref, o_ref, hp_ref):
    _rms_rows(h_ref, g_ref, hp_ref, min(64, h_ref.shape[0]))
    t = jnp.dot(hp_ref[...], wgd_ref[...], preferred_element_type=F32).astype(BF16)
    gate = jax.nn.sigmoid(jnp.dot(t, wgu_ref[...], preferred_element_type=F32))
    e = jnp.dot(p_ref[...].astype(BF16), wple_ref[...], preferred_element_type=F32)
    o_ref[...] = h_ref[...] + e * gate


def _ple(h, gain, p, wple, wgd, wgu, layer, *, bm):
    M, D = h.shape
    P = p.shape[2]
    R = wgd.shape[2]
    est = 2 * (2 * bm * D * 4 + bm * P * 4 + (P + 2 * R) * D * 2) + bm * D * 2 + 4 * bm * D * 4
    return pl.pallas_call(
        _ple_kernel,
        out_shape=jax.ShapeDtypeStruct((M, D), F32),
        grid=(M // bm,),
        in_specs=[pl.BlockSpec((bm, D), lambda i: (i, 0)),
                  pl.BlockSpec((1, D), lambda i: (0, 0)),
                  pl.BlockSpec((None, bm, P), lambda i: (layer, i, 0)),
                  pl.BlockSpec((None, P, D), lambda i: (layer, 0, 0)),
                  pl.BlockSpec((None, D, R), lambda i: (layer, 0, 0)),
                  pl.BlockSpec((None, R, D), lambda i: (layer, 0, 0))],
        out_specs=pl.BlockSpec((bm, D), lambda i: (i, 0)),
        scratch_shapes=[pltpu.VMEM((bm, D), BF16)],
        input_output_aliases={0: 0},
        compiler_params=_cparams(1, est),
        name="ple",
    )(h, gain.reshape(1, D), p, wple, wgd, wgu)


def _final_norm_kernel(x_ref, g_ref, o_ref):
    _rms_rows(x_ref, g_ref, o_ref, min(64, x_ref.shape[0]))


def _final_norm(h, gain, *, bm):
    M, D = h.shape
    return pl.pallas_call(
        _final_norm_kernel,
        out_shape=jax.ShapeDtypeStruct((M, D), F32),
        grid=(M // bm,),
        in_specs=[pl.BlockSpec((bm, D), lambda i: (i, 0)), pl.BlockSpec((1, D), lambda i: (0, 0))],
        out_specs=pl.BlockSpec((bm, D), lambda i: (i, 0)),
        compiler_params=_cparams(1, 4 * bm * D * 4 + 64 * D * 4 * 4),
        name="final_norm",
    )(h, gain.reshape(1, D))


def kernel(x, p, rel_bias, norm_mix, w_in, diff_lambda, diff_subln, s5_lambda_re, s5_lambda_im, s5_log_dt, s5_b_re, s5_b_im, s5_c_re, s5_c_im, s5_d, s5_w_glu, w_gate_up, w_br_diff, w_br_ret, w_br_s5, w_o, norm_ffn, w_ffn_gate, w_ffn_up, w_ffn_down, norm_ple, w_ple, w_ple_gate_down, w_ple_gate_up, norm_final):
    B, S, D = x.shape
    assert B == 1
    depth = w_in.shape[0]
    diff_w, ret_w, s5_w = w_br_diff.shape[1], w_br_ret.shape[1], w_br_s5.shape[1]
    gate_rank = w_gate_up.shape[1]
    diff_heads = diff_w // (2 * HEAD_DIM)
    ret_heads = ret_w // HEAD_DIM
    off_dq = 0
    off_dk = off_dq + diff_w
    off_dv = off_dk + diff_w
    off_rq = off_dv + diff_w
    off_rk = off_rq + ret_w
    off_rv = off_rk + ret_w
    off_rg = off_rv + ret_w
    off_su = off_rg + ret_w
    off_gate = off_su + s5_w
    in_w = off_gate + gate_rank
    assert w_in.shape[2] == in_w
    hidden = w_ffn_gate.shape[2]

    T_attn = _tile(S, 1024)
    T_ret = _tile(S, 512)
    T_s5 = _tile(S, 256)
    bm_in = _tile(S, 512)
    bn_in = _tile(in_w, 768) if in_w % 768 == 0 else _tile(in_w, 512)
    bm_mg, bn_mg = _tile(S, 1024), _tile(D, 512)
    bm_o, bn_o = _tile(S, 1024), _tile(D, 512)
    bm_up, bn_up = _tile(S, 512), min(hidden, 512)
    bm_dn, bn_dn = _tile(S, 512), _tile(D, 256)
    bm_ple = _tile(S, 256)

    h = x.reshape(S, D)
    bias_tiles = _bias_tiles(rel_bias, diff_heads, T_attn)
    ret_tables = _retention_tables(S, ret_heads)
    z_scale = jnp.where(jnp.arange(in_w) < off_dk, HEAD_DIM ** -0.5 * LOG2E, 1.0).astype(F32)

    (w_in_b, w_glu_b, w_gate_b, w_brd_b, w_brr_b, w_brs_b, w_o_b, w_fg_b, w_fu_b, w_fd_b, w_ple_b,
     w_pgd_b, w_pgu_b) = (w.astype(BF16) for w in (
         w_in, s5_w_glu, w_gate_up, w_br_diff, w_br_ret, w_br_s5, w_o, w_ffn_gate, w_ffn_up,
         w_ffn_down, w_ple, w_ple_gate_down, w_ple_gate_up))
    p3 = p.reshape(depth, S, p.shape[-1])

    for i in range(depth):
        lam_init = 0.8 - 0.6 * math.exp(-0.3 * i)
        z = _norm_mm(h, norm_mix[i], w_in_b, i, z_scale, bm=bm_in, bn=bn_in)
        o_diff = _diff_attention(z, rel_bias, bias_tiles, diff_lambda[i], diff_subln[i], lam_init,
                                 off_q=off_dq, off_k=off_dk, off_v=off_dv, n_heads=diff_heads, T=T_attn)
        o_ret = _retention(z, ret_tables, off_q=off_rq, off_k=off_rk, off_v=off_rv, off_g=off_rg,
                           n_heads=ret_heads, TR=T_ret)
        s5p = _s5_params(s5_lambda_re[i], s5_lambda_im[i], s5_log_dt[i], s5_b_re[i], s5_b_im[i],
                         s5_c_re[i], s5_c_im[i])
        o_s5 = _s5(z, s5p, s5_d[i], w_glu_b, i, off_u=off_su, width=s5_w, Tt=T_s5)
        mixed = _merge(z, o_diff, o_ret, o_s5, w_gate_b, w_brd_b, w_brr_b, w_brs_b, i,
                       off_gate=off_gate, bm=bm_mg, bn=bn_mg)
        h = _mm_res(mixed, w_o_b, i, h, bm=bm_o, bn=bn_o)
        act = _ffn_up(h, norm_ffn[i], w_fg_b, w_fu_b, i, bm=bm_up, bn=bn_up)
        h = _mm_res(act, w_fd_b, i, h, bm=bm_dn, bn=bn_dn)
        h = _ple(h, norm_ple[i], p3, w_ple_b, w_pgd_b, w_pgu_b, i, bm=bm_ple)
    out = _final_norm(h, norm_final, bm=_tile(S, 256))
    return out.reshape(B, S, D)
```

```python
import functools
import math

import jax
import jax.numpy as jnp
from jax import lax
from jax.experimental import pallas as pl
from jax.experimental.pallas import tpu as pltpu

F32 = jnp.float32
BF16 = jnp.bfloat16
EPS = 1e-6

HEAD_DIM = 128
CHUNK = 128
REL_BUCKETS = 32
REL_MAX_DIST = 128
S5_GROUP = 16
S5_STATE = 64
S5_SLICE_GROUPS = 8
MASK_VALUE = -1e30
LOG2E = math.log2(math.e)
ATTN_ROW_CHUNK = 128
ATTN_HEADS_PER_STEP = 2

V7X_VMEM_BYTES = 64 * 1024 * 1024
V7X_VMEM_REQUEST_CAP = 60 * 1024 * 1024
SPILL_AND_TEMP_ALLOWANCE = 8 * 1024 * 1024


def _cparams(n_axes, est_bytes):
    limit = int(min(est_bytes + SPILL_AND_TEMP_ALLOWANCE, V7X_VMEM_REQUEST_CAP))
    return pltpu.CompilerParams(dimension_semantics=("arbitrary",) * n_axes, vmem_limit_bytes=limit)


def _tile(n, want):
    t = min(n, want)
    while n % t:
        t //= 2
    return t


def _rms_rows(x_ref, g_ref, dst_ref, rows):
    n = x_ref.shape[0] // rows

    def body(i, c):
        r = pl.multiple_of(i * rows, rows)
        x = x_ref[pl.ds(r, rows), :]
        ms = jnp.mean(x * x, axis=-1, keepdims=True)
        dst_ref[pl.ds(r, rows), :] = (x * lax.rsqrt(ms + EPS) * g_ref[...]).astype(dst_ref.dtype)
        return c

    lax.fori_loop(0, n, body, 0)


def _norm_mm_kernel(x_ref, g_ref, w_ref, cs_ref, o_ref, xn_ref):
    @pl.when(pl.program_id(1) == 0)
    def _():
        _rms_rows(x_ref, g_ref, xn_ref, min(64, x_ref.shape[0]))

    acc = jnp.dot(xn_ref[...], w_ref[...], preferred_element_type=F32)
    o_ref[...] = (acc * cs_ref[...]).astype(o_ref.dtype)


def _wspec(layer, k, bn, col_block=lambda j: j):
    return pl.BlockSpec((None, k, bn), lambda i, j: (layer, 0, col_block(j)))


def _norm_mm(h, gain, w, layer, col_scale, *, bm, bn, out_dtype=BF16):
    M, K = h.shape
    N = w.shape[2]
    est = 2 * (bm * K * 4 + K * bn * 2 + bm * bn * 4) + bm * K * 2 + bm * bn * 4
    return pl.pallas_call(
        _norm_mm_kernel,
        out_shape=jax.ShapeDtypeStruct((M, N), out_dtype),
        grid=(M // bm, N // bn),
        in_specs=[pl.BlockSpec((bm, K), lambda i, j: (i, 0)),
                  pl.BlockSpec((1, K), lambda i, j: (0, 0)),
                  _wspec(layer, K, bn),
                  pl.BlockSpec((1, bn), lambda i, j: (0, j))],
        out_specs=pl.BlockSpec((bm, bn), lambda i, j: (i, j)),
        scratch_shapes=[pltpu.VMEM((bm, K), BF16)],
        compiler_params=_cparams(2, est),
        name="norm_mm",
    )(h, gain.reshape(1, K), w, col_scale.reshape(1, N))


def _bias_tile_kernel(rb_ref, o_ref):
    h = pl.program_id(0)
    d = pl.program_id(1)
    T = o_ref.shape[2]
    r = lax.broadcasted_iota(jnp.int32, (T, T), 0)
    c = lax.broadcasted_iota(jnp.int32, (T, T), 1)
    rel = d * T + r - c
    n = jnp.maximum(rel, 0)
    max_exact = REL_BUCKETS // 2
    nf = jnp.maximum(n, 1).astype(F32)
    large = max_exact + (jnp.log(nf / max_exact) / math.log(REL_MAX_DIST / max_exact)
                         * (REL_BUCKETS - max_exact)).astype(jnp.int32)
    large = jnp.minimum(large, REL_BUCKETS - 1)
    bucket = jnp.where(n < max_exact, n, large)
    val = jnp.zeros((T, T), F32)
    for b in range(REL_BUCKETS):
        val = jnp.where(bucket == b, rb_ref[b, h], val)
    o_ref[0, 0] = jnp.where(rel >= 0, val * LOG2E, MASK_VALUE)


def _bias_tiles(rel_bias, n_heads, T):
    return pl.pallas_call(
        _bias_tile_kernel,
        out_shape=jax.ShapeDtypeStruct((n_heads, 2, T, T), F32),
        grid=(n_heads, 2),
        in_specs=[pl.BlockSpec(memory_space=pltpu.SMEM)],
        out_specs=pl.BlockSpec((1, 1, T, T), lambda h, d: (h, d, 0, 0)),
        compiler_params=_cparams(2, 12 * T * T * 4),
        name="t5_bias_tiles",
    )(rel_bias)


def _attn_kernel(qt_ref, kt_ref, rb_ref, li_ref, dl_ref, sub_ref, q_ref, k_ref, v_ref, bt_ref,
                 o_ref, m_ref, l_ref, acc_ref, *, row_chunks):
    hg = pl.program_id(0)
    p = pl.program_id(1)
    qi = qt_ref[p]
    ki = kt_ref[p]
    T = q_ref.shape[0]
    W = 2 * HEAD_DIM
    hb = q_ref.shape[1] // W

    @pl.when(ki == 0)
    def _():
        m_ref[...] = jnp.full(m_ref.shape, MASK_VALUE, F32)
        l_ref[...] = jnp.zeros(l_ref.shape, F32)
        acc_ref[...] = jnp.zeros(acc_ref.shape, F32)

    def step(tile, causal):
        R = T // row_chunks
        for r in range(row_chunks):
            rows = slice(r * R, (r + 1) * R)
            ncol = (r + 1) * R if causal else T
            for hh in range(hb):
                v = v_ref[0:ncol, hh * W:(hh + 1) * W]
                bias_const = None if tile is not None else rb_ref[REL_BUCKETS - 1, hg * hb + hh] * LOG2E
                for c in range(2):
                    pc = 2 * hh + c
                    col0 = hh * W + c * HEAD_DIM
                    qc = q_ref[rows, col0:col0 + HEAD_DIM]
                    kc = k_ref[0:ncol, col0:col0 + HEAD_DIM]
                    s = lax.dot_general(qc, kc, (((1,), (1,)), ((), ())), preferred_element_type=F32)
                    if tile is not None:
                        s = s + bt_ref[hh, tile, rows, 0:ncol]
                    m_prev = m_ref[pc, rows, :]
                    row_max = jnp.max(s, axis=-1, keepdims=True)
                    if bias_const is not None:
                        row_max = row_max + bias_const
                    m_new = jnp.maximum(m_prev, row_max)
                    alpha = jnp.exp2(m_prev - m_new)
                    shift = m_new if bias_const is None else m_new - bias_const
                    l_part = alpha * l_ref[pc, rows, :]
                    ps = []
                    for j in range(ncol // 128):
                        pj = jnp.exp2(s[:, j * 128:(j + 1) * 128] - shift)
                        l_part = l_part + pj
                        ps.append(pj.astype(BF16))
                    l_ref[pc, rows, :] = l_part
                    m_ref[pc, rows, :] = m_new
                    pv = jnp.dot(jnp.concatenate(ps, axis=1), v, preferred_element_type=F32)
                    acc_ref[pc, rows, :] = (jnp.concatenate([alpha] * (W // 128), axis=1)
                                            * acc_ref[pc, rows, :] + pv)

    @pl.when(qi - ki >= 2)
    def _():
        step(None, False)

    @pl.when(qi - ki == 1)
    def _():
        step(1, False)

    @pl.when(qi == ki)
    def _():
        step(0, True)
        lam_init = li_ref[0]
        dl = dl_ref[...]
        lam = (jnp.exp(jnp.sum(dl[0:1] * dl[1:2], axis=-1, keepdims=True))
               - jnp.exp(jnp.sum(dl[2:3] * dl[3:4], axis=-1, keepdims=True)) + lam_init)
        for hh in range(hb):
            l0 = jnp.sum(l_ref[2 * hh], axis=-1, keepdims=True)
            l1 = jnp.sum(l_ref[2 * hh + 1], axis=-1, keepdims=True)
            o = acc_ref[2 * hh] / l0 - lam * (acc_ref[2 * hh + 1] / l1)
            ms = jnp.mean(o * o, axis=-1, keepdims=True)
            y = (o * lax.rsqrt(ms + EPS) * sub_ref[...]) * (1.0 - lam_init)
            o_ref[:, hh * W:(hh + 1) * W] = y.astype(o_ref.dtype)


def _diff_attention(z, rel_bias, bias_tiles, diff_lambda, subln, lam_init, *, off_q, off_k, off_v,
                    n_heads, T):
    S = z.shape[0]
    W = 2 * HEAD_DIM
    nq = S // T
    qs, ks = [], []
    for qi in range(nq):
        for ki in range(qi + 1):
            qs.append(qi)
            ks.append(ki)
    qt = jnp.asarray(qs, jnp.int32)
    kt = jnp.asarray(ks, jnp.int32)
    hb = ATTN_HEADS_PER_STEP if n_heads % ATTN_HEADS_PER_STEP == 0 else 1
    WB = hb * W
    assert off_q % WB == 0 and off_k % WB == 0 and off_v % WB == 0
    bq, bk, bv = off_q // WB, off_k // WB, off_v // WB
    grid_spec = pltpu.PrefetchScalarGridSpec(
        num_scalar_prefetch=2,
        grid=(n_heads // hb, len(qs)),
        in_specs=[pl.BlockSpec(memory_space=pltpu.SMEM),
                  pl.BlockSpec(memory_space=pltpu.SMEM),
                  pl.BlockSpec((4, HEAD_DIM), lambda h, p, qt, kt: (0, 0)),
                  pl.BlockSpec((1, W), lambda h, p, qt, kt: (0, 0)),
                  pl.BlockSpec((T, WB), lambda h, p, qt, kt: (qt[p], bq + h)),
                  pl.BlockSpec((T, WB), lambda h, p, qt, kt: (kt[p], bk + h)),
                  pl.BlockSpec((T, WB), lambda h, p, qt, kt: (kt[p], bv + h)),
                  pl.BlockSpec((hb, 2, T, T), lambda h, p, qt, kt: (h, 0, 0, 0),
                               pipeline_mode=pl.Buffered(1))],
        out_specs=pl.BlockSpec((T, WB), lambda h, p, qt, kt: (qt[p], h)),
        scratch_shapes=[pltpu.VMEM((2 * hb, T, 128), F32), pltpu.VMEM((2 * hb, T, 128), F32),
                        pltpu.VMEM((2 * hb, T, W), F32)],
    )
    est = (2 * 4 * T * WB * 2 + hb * 2 * T * T * 4 + 2 * hb * T * W * 4 + 4 * hb * T * 128 * 4
           + 16 * ATTN_ROW_CHUNK * T * 4)
    return pl.pallas_call(
        functools.partial(_attn_kernel, row_chunks=max(1, T // ATTN_ROW_CHUNK)),
        out_shape=jax.ShapeDtypeStruct((S, n_heads * W), BF16),
        grid_spec=grid_spec,
        compiler_params=_cparams(2, est),
        name="diff_attention",
    )(qt, kt, rel_bias, jnp.full((1,), lam_init, F32), diff_lambda, subln.reshape(1, W),
      z, z, z, bias_tiles)


def _ret_kernel(cd_ref, q_ref, k_ref, v_ref, g_ref, cos_ref, sin_ref, dm_ref, xi_ref, ze_ref,
                o_ref, R_ref):
    t = pl.program_id(0)
    n_heads = R_ref.shape[0]

    @pl.when(t == 0)
    def _():
        R_ref[...] = jnp.zeros(R_ref.shape, F32)

    half = HEAD_DIM // 2
    Rs = [R_ref[h] for h in range(n_heads)]
    for c in range(q_ref.shape[0] // CHUNK):
        sl = slice(c * CHUNK, (c + 1) * CHUNK)
        cs = cos_ref[sl, :]
        sn = sin_ref[sl, :]
        for h in range(n_heads):
            hc = slice(h * HEAD_DIM, (h + 1) * HEAD_DIM)
            q = q_ref[sl, hc].astype(F32)
            k = k_ref[sl, hc].astype(F32)
            v = v_ref[sl, hc]
            qr = q * cs + pltpu.roll(q, half, 1) * sn
            kr = (k * cs + pltpu.roll(k, half, 1) * sn) * (HEAD_DIM ** -0.5)
            qb = qr.astype(BF16)
            inner = lax.dot_general(qb, kr.astype(BF16), (((1,), (1,)), ((), ())),
                                    preferred_element_type=F32) * dm_ref[h]
            o = (jnp.dot(inner.astype(BF16), v, preferred_element_type=F32)
                 + jnp.dot(qb, Rs[h].astype(BF16), preferred_element_type=F32) * xi_ref[h])
            Rs[h] = Rs[h] * cd_ref[h] + lax.dot_general(
                (kr * ze_ref[h]).astype(BF16), v, (((0,), (0,)), ((), ())), preferred_element_type=F32)
            o = o * lax.rsqrt(jnp.mean(o * o, axis=-1, keepdims=True) + EPS)
            g = g_ref[sl, hc].astype(F32)
            o_ref[sl, hc] = (jax.nn.silu(g) * o).astype(o_ref.dtype)
    for h in range(n_heads):
        R_ref[h] = Rs[h]


def _retention_tables(S, n_heads):
    half = HEAD_DIM // 2
    pos = jnp.arange(S, dtype=F32)
    theta = 1.0 / (10000.0 ** jnp.linspace(0.0, 1.0, half, dtype=F32))
    ang = pos[:, None] * theta[None, :]
    cos = jnp.cos(ang)
    sin = jnp.sin(ang)
    cosf = jnp.concatenate([cos, cos], axis=-1)
    sinf = jnp.concatenate([-sin, sin], axis=-1)
    log_gamma = jnp.log1p(-jnp.exp2(-5.0 - jnp.arange(n_heads, dtype=F32)))
    idx = jnp.arange(CHUNK)
    diff = idx[:, None] - idx[None, :]
    dmat = jnp.where(diff >= 0,
                     jnp.exp(log_gamma[:, None, None] * jnp.maximum(diff, 0).astype(F32)), 0.0)
    xi = jnp.exp(log_gamma[:, None] * (idx[None, :] + 1).astype(F32))[..., None]
    zeta = jnp.exp(log_gamma[:, None] * (CHUNK - 1 - idx)[None, :].astype(F32))[..., None]
    chunk_decay = jnp.exp(log_gamma * CHUNK)
    return cosf, sinf, dmat, xi, zeta, chunk_decay


def _retention(z, tables, *, off_q, off_k, off_v, off_g, n_heads, TR):
    S = z.shape[0]
    cosf, sinf, dmat, xi, zeta, chunk_decay = tables
    WR = n_heads * HEAD_DIM
    assert all(o % WR == 0 for o in (off_q, off_k, off_v, off_g))
    blk = lambda off: pl.BlockSpec((TR, WR), lambda t: (t, off // WR))
    tab = pl.BlockSpec((TR, HEAD_DIM), lambda t: (t, 0))
    whole = lambda last: pl.BlockSpec((n_heads, CHUNK, last), lambda t: (0, 0, 0))
    est = (2 * (5 * TR * WR * 2 + 2 * TR * HEAD_DIM * 4 + n_heads * 3 * CHUNK * CHUNK * 4)
           + n_heads * 16 * CHUNK * CHUNK * 4)
    return pl.pallas_call(
        _ret_kernel,
        out_shape=jax.ShapeDtypeStruct((S, WR), BF16),
        grid=(S // TR,),
        in_specs=[pl.BlockSpec(memory_space=pltpu.SMEM),
                  blk(off_q), blk(off_k), blk(off_v), blk(off_g), tab, tab,
                  whole(CHUNK), whole(1), whole(1)],
        out_specs=pl.BlockSpec((TR, WR), lambda t: (t, 0)),
        scratch_shapes=[pltpu.VMEM((n_heads, HEAD_DIM, HEAD_DIM), F32)],
        compiler_params=_cparams(1, est),
        name="retention",
    )(chunk_decay, z, z, z, z, cosf, sinf, dmat, xi, zeta)


def _s5_kernel(u_ref, benc_ref, apr_ref, api_ref, cdec_ref, d_ref, wglu_ref, o_ref,
               carry_ref, z_ref):
    t = pl.program_id(0)
    j = pl.program_id(1)
    nsl = pl.num_programs(1)
    Tt = u_ref.shape[0]
    C = apr_ref.shape[2]

    @pl.when((t == 0) & (j == 0))
    def _():
        carry_ref[...] = jnp.zeros(carry_ref.shape, F32)

    u = u_ref[...]
    x = jnp.dot(u, benc_ref[0], preferred_element_type=F32)
    ng = Tt // 8
    hr = x[:, :C].reshape(ng, 8, C)
    hi = x[:, C:].reshape(ng, 8, C)
    apr = apr_ref[0]
    api = api_ref[0]
    sub = lax.broadcasted_iota(jnp.int32, (ng, 8, C), 1)
    for d in (1, 2, 4):
        a_r = apr[d - 1:d, :].reshape(1, 1, C)
        a_i = api[d - 1:d, :].reshape(1, 1, C)
        sr = pltpu.roll(hr, d, 1)
        si = pltpu.roll(hi, d, 1)
        m = sub >= d
        nr = hr + jnp.where(m, a_r * sr - a_i * si, 0.0)
        ni = hi + jnp.where(m, a_r * si + a_i * sr, 0.0)
        hr, hi = nr, ni
    cr = carry_ref[j]
    c_r = cr[:, :C]
    c_i = cr[:, C:]
    out_r, out_i = [], []
    for g in range(ng):
        b_r = jnp.broadcast_to(c_r, (8, C))
        b_i = jnp.broadcast_to(c_i, (8, C))
        g_r = hr[g] + (apr * b_r - api * b_i)
        g_i = hi[g] + (apr * b_i + api * b_r)
        out_r.append(g_r)
        out_i.append(g_i)
        c_r = g_r[7:8, :]
        c_i = g_i[7:8, :]
    carry_ref[j] = jnp.concatenate([c_r, c_i], axis=1)
    hcat = jnp.concatenate([jnp.concatenate(out_r, axis=0), jnp.concatenate(out_i, axis=0)],
                           axis=1).astype(BF16)
    y = jnp.dot(hcat, cdec_ref[0], preferred_element_type=F32) + d_ref[0] * u.astype(F32)
    z_ref[j] = jax.nn.gelu(y).astype(BF16)

    @pl.when(j == nsl - 1)
    def _():
        zc = jnp.concatenate([z_ref[s] for s in range(z_ref.shape[0])], axis=1)
        zg = jnp.dot(zc, wglu_ref[...], preferred_element_type=F32)
        W = zg.shape[1] // 2
        o_ref[...] = (zg[:, :W] * jax.nn.sigmoid(zg[:, W:])).astype(o_ref.dtype)


def _s5_params(lam_re, lam_im, log_dt, b_re, b_im, c_re, c_im):
    G, N = lam_re.shape
    sg = S5_SLICE_GROUPS
    nsl = G // sg
    lr, li = lam_re.astype(F32), lam_im.astype(F32)
    dt = jnp.exp(log_dt.astype(F32))[:, None]
    mag = jnp.exp(lr * dt)
    ar = mag * jnp.cos(li * dt)
    ai = mag * jnp.sin(li * dt)
    den = lr * lr + li * li
    fr = ((ar - 1.0) * lr + ai * li) / den
    fi = (ai * lr - (ar - 1.0) * li) / den
    br_, bi_ = b_re.astype(F32), b_im.astype(F32)
    bbr = fr[..., None] * br_ - fi[..., None] * bi_
    bbi = fr[..., None] * bi_ + fi[..., None] * br_
    eye = jnp.eye(sg, dtype=F32)

    def enc(bb):
        return jnp.einsum('sgnj,gh->sgjhn', bb.reshape(nsl, sg, N, S5_GROUP), eye).reshape(
            nsl, sg * S5_GROUP, sg * N)

    def dec(cc):
        return jnp.einsum('sgjn,gh->sgnhj', cc.reshape(nsl, sg, S5_GROUP, N), eye).reshape(
            nsl, sg * N, sg * S5_GROUP)

    benc = jnp.concatenate([enc(bbr), enc(bbi)], axis=-1).astype(BF16)
    cdec = jnp.concatenate([dec(c_re.astype(F32)), -dec(c_im.astype(F32))], axis=1).astype(BF16)
    prs, pis = [ar], [ai]
    for _ in range(7):
        pr, pi = prs[-1], pis[-1]
        prs.append(pr * ar - pi * ai)
        pis.append(pr * ai + pi * ar)
    apr = jnp.stack(prs).reshape(8, nsl, sg * N).transpose(1, 0, 2)
    api = jnp.stack(pis).reshape(8, nsl, sg * N).transpose(1, 0, 2)
    return benc, cdec, apr, api


def _s5(z, s5p, d_skip, wglu, layer, *, off_u, width, Tt):
    S = z.shape[0]
    benc, cdec, apr, api = s5p
    nsl, uw, xw = benc.shape
    C = xw // 2
    lp = apr.shape[1]
    bu = off_u // uw
    est = (2 * (Tt * uw * 2 + uw * xw * 2 + 2 * lp * C * 4 + xw * uw * 2 + width * 2 * width * 2
                + Tt * width * 2) + nsl * Tt * uw * 2 + 24 * Tt * C * 4 + Tt * 2 * width * 4 * 2)
    return pl.pallas_call(
        _s5_kernel,
        out_shape=jax.ShapeDtypeStruct((S, width), BF16),
        grid=(S // Tt, nsl),
        in_specs=[pl.BlockSpec((Tt, uw), lambda t, j: (t, bu + j)),
                  pl.BlockSpec((1, uw, xw), lambda t, j: (j, 0, 0)),
                  pl.BlockSpec((1, lp, C), lambda t, j: (j, 0, 0)),
                  pl.BlockSpec((1, lp, C), lambda t, j: (j, 0, 0)),
                  pl.BlockSpec((1, xw, uw), lambda t, j: (j, 0, 0)),
                  pl.BlockSpec((1, 1, uw), lambda t, j: (j, 0, 0)),
                  pl.BlockSpec((None, width, 2 * width), lambda t, j: (layer, 0, 0))],
        out_specs=pl.BlockSpec((Tt, width), lambda t, j: (t, 0)),
        scratch_shapes=[pltpu.VMEM((nsl, 1, xw), F32), pltpu.VMEM((nsl, Tt, uw), BF16)],
        compiler_params=_cparams(2, est),
        name="s5",
    )(z, benc, apr, api, cdec, d_skip.astype(F32).reshape(nsl, 1, uw), wglu)


def _merge_kernel(gl_ref, od_ref, or_ref, os_ref, wg0_ref, wg1_ref, wg2_ref, wd_ref, wr_ref, ws_ref,
                  o_ref):
    gl = gl_ref[...]

    def branch(wg_ref, x_ref, w_ref):
        gate = jax.nn.sigmoid(jnp.dot(gl, wg_ref[...], preferred_element_type=F32))
        return gate * jnp.dot(x_ref[...], w_ref[...], preferred_element_type=F32)

    o_ref[...] = (branch(wg0_ref, od_ref, wd_ref) + branch(wg1_ref, or_ref, wr_ref)
                  + branch(wg2_ref, os_ref, ws_ref)).astype(o_ref.dtype)


def _merge(z, o_diff, o_ret, o_s5, wg, wd, wr, ws, layer, *, off_gate, bm, bn):
    S = z.shape[0]
    R = wg.shape[1]
    D = wd.shape[2]
    nb = D // bn
    kd, kr, ks = wd.shape[1], wr.shape[1], ws.shape[1]
    row = lambda k: pl.BlockSpec((bm, k), lambda i, j: (i, 0))
    col = lambda k: _wspec(layer, k, bn)
    est = 2 * (bm * (R + kd + kr + ks) * 2 + (3 * R + kd + kr + ks) * bn * 2 + bm * bn * 2) + 8 * bm * bn * 4
    return pl.pallas_call(
        _merge_kernel,
        out_shape=jax.ShapeDtypeStruct((S, D), BF16),
        grid=(S // bm, nb),
        in_specs=[pl.BlockSpec((bm, R), lambda i, j: (i, off_gate // R)),
                  row(kd), row(kr), row(ks),
                  _wspec(layer, R, bn),
                  _wspec(layer, R, bn, lambda j: nb + j),
                  _wspec(layer, R, bn, lambda j: 2 * nb + j),
                  col(kd), col(kr), col(ks)],
        out_specs=pl.BlockSpec((bm, bn), lambda i, j: (i, j)),
        compiler_params=_cparams(2, est),
        name="gated_merge",
    )(z, o_diff, o_ret, o_s5, wg, wg, wg, wd, wr, ws)


def _mm_res_kernel(x_ref, w_ref, h_ref, o_ref):
    o_ref[...] = h_ref[...] + jnp.dot(x_ref[...], w_ref[...], preferred_element_type=F32)


def _mm_res(x, w, layer, h, *, bm, bn):
    M, K = x.shape
    N = w.shape[2]
    est = 2 * (bm * K * 2 + K * bn * 2 + 2 * bm * bn * 4) + bm * bn * 4
    return pl.pallas_call(
        _mm_res_kernel,
        out_shape=jax.ShapeDtypeStruct((M, N), F32),
        grid=(M // bm, N // bn),
        in_specs=[pl.BlockSpec((bm, K), lambda i, j: (i, 0)),
                  _wspec(layer, K, bn),
                  pl.BlockSpec((bm, bn), lambda i, j: (i, j))],
        out_specs=pl.BlockSpec((bm, bn), lambda i, j: (i, j)),
        input_output_aliases={2: 0},
        compiler_params=_cparams(2, est),
        name="mm_residual",
    )(x, w, h)


def _ffn_up_kernel(x_ref, g_ref, wg_ref, wu_ref, o_ref, xn_ref):
    @pl.when(pl.program_id(1) == 0)
    def _():
        _rms_rows(x_ref, g_ref, xn_ref, min(64, x_ref.shape[0]))

    xn = xn_ref[...]
    a = jnp.dot(xn, wg_ref[...], preferred_element_type=F32)
    b = jnp.dot(xn, wu_ref[...], preferred_element_type=F32)
    o_ref[...] = (jax.nn.silu(a) * b).astype(o_ref.dtype)


def _ffn_up(h, gain, wg, wu, layer, *, bm, bn):
    M, K = h.shape
    N = wg.shape[2]
    est = 2 * (bm * K * 4 + 2 * K * bn * 2 + bm * bn * 2) + bm * K * 2 + 4 * bm * bn * 4
    return pl.pallas_call(
        _ffn_up_kernel,
        out_shape=jax.ShapeDtypeStruct((M, N), BF16),
        grid=(M // bm, pl.cdiv(N, bn)),
        in_specs=[pl.BlockSpec((bm, K), lambda i, j: (i, 0)),
                  pl.BlockSpec((1, K), lambda i, j: (0, 0)),
                  _wspec(layer, K, bn),
                  _wspec(layer, K, bn)],
        out_specs=pl.BlockSpec((bm, bn), lambda i, j: (i, j)),
        scratch_shapes=[pltpu.VMEM((bm, K), BF16)],
        compiler_params=_cparams(2, est),
        name="ffn_up",
    )(h, gain.reshape(1, K), wg, wu)


def _ple_kernel(h_ref, g_ref, p_ref, wple_ref, wgd_ref, wgu_ref, o_ref, hp_ref):
    _rms_rows(h_ref, g_ref, hp_ref, min(64, h_ref.shape[0]))
    t = jnp.dot(hp_ref[...], wgd_ref[...], preferred_element_type=F32).astype(BF16)
    gate = jax.nn.sigmoid(jnp.dot(t, wgu_ref[...], preferred_element_type=F32))
    e = jnp.dot(p_ref[...].astype(BF16), wple_ref[...], preferred_element_type=F32)
    o_ref[...] = h_ref[...] + e * gate


def _ple(h, gain, p, wple, wgd, wgu, layer, *, bm):
    M, D = h.shape
    P = p.shape[2]
    R = wgd.shape[2]
    est = 2 * (2 * bm * D * 4 + bm * P * 4 + (P + 2 * R) * D * 2) + bm * D * 2 + 4 * bm * D * 4
    return pl.pallas_call(
        _ple_kernel,
        out_shape=jax.ShapeDtypeStruct((M, D), F32),
        grid=(M // bm,),
        in_specs=[pl.BlockSpec((bm, D), lambda i: (i, 0)),
                  pl.BlockSpec((1, D), lambda i: (0, 0)),
                  pl.BlockSpec((None, bm, P), lambda i: (layer, i, 0)),
                  pl.BlockSpec((None, P, D), lambda i: (layer, 0, 0)),
                  pl.BlockSpec((None, D, R), lambda i: (layer, 0, 0)),
                  pl.BlockSpec((None, R, D), lambda i: (layer, 0, 0))],
        out_specs=pl.BlockSpec((bm, D), lambda i: (i, 0)),
        scratch_shapes=[pltpu.VMEM((bm, D), BF16)],
        input_output_aliases={0: 0},
        compiler_params=_cparams(1, est),
        name="ple",
    )(h, gain.reshape(1, D), p, wple, wgd, wgu)


def _final_norm_kernel(x_ref, g_ref, o_ref):
    _rms_rows(x_ref, g_ref, o_ref, min(64, x_ref.shape[0]))


def _final_norm(h, gain, *, bm):
    M, D = h.shape
    return pl.pallas_call(
        _final_norm_kernel,
        out_shape=jax.ShapeDtypeStruct((M, D), F32),
        grid=(M // bm,),
        in_specs=[pl.BlockSpec((bm, D), lambda i: (i, 0)), pl.BlockSpec((1, D), lambda i: (0, 0))],
        out_specs=pl.BlockSpec((bm, D), lambda i: (i, 0)),
        compiler_params=_cparams(1, 4 * bm * D * 4 + 64 * D * 4 * 4),
        name="final_norm",
    )(h, gain.reshape(1, D))


def kernel(x, p, rel_bias, norm_mix, w_in, diff_lambda, diff_subln, s5_lambda_re, s5_lambda_im, s5_log_dt, s5_b_re, s5_b_im, s5_c_re, s5_c_im, s5_d, s5_w_glu, w_gate_up, w_br_diff, w_br_ret, w_br_s5, w_o, norm_ffn, w_ffn_gate, w_ffn_up, w_ffn_down, norm_ple, w_ple, w_ple_gate_down, w_ple_gate_up, norm_final):
    B, S, D = x.shape
    assert B == 1
    depth = w_in.shape[0]
    diff_w, ret_w, s5_w = w_br_diff.shape[1], w_br_ret.shape[1], w_br_s5.shape[1]
    gate_rank = w_gate_up.shape[1]
    diff_heads = diff_w // (2 * HEAD_DIM)
    ret_heads = ret_w // HEAD_DIM
    off_dq = 0
    off_dk = off_dq + diff_w
    off_dv = off_dk + diff_w
    off_rq = off_dv + diff_w
    off_rk = off_rq + ret_w
    off_rv = off_rk + ret_w
    off_rg = off_rv + ret_w
    off_su = off_rg + ret_w
    off_gate = off_su + s5_w
    in_w = off_gate + gate_rank
    assert w_in.shape[2] == in_w
    hidden = w_ffn_gate.shape[2]

    T_attn = _tile(S, 1024)
    T_ret = _tile(S, 256)
    T_s5 = _tile(S, 256)
    bm_in = _tile(S, 512)
    bn_in = _tile(in_w, 768) if in_w % 768 == 0 else _tile(in_w, 512)
    bm_mg, bn_mg = _tile(S, 1024), _tile(D, 512)
    bm_o, bn_o = _tile(S, 1024), _tile(D, 512)
    bm_up, bn_up = _tile(S, 512), min(hidden, 512)
    bm_dn, bn_dn = _tile(S, 512), _tile(D, 256)
    bm_ple = _tile(S, 256)

    h = x.reshape(S, D)
    bias_tiles = _bias_tiles(rel_bias, diff_heads, T_attn)
    ret_tables = _retention_tables(S, ret_heads)
    z_scale = jnp.where(jnp.arange(in_w) < off_dk, HEAD_DIM ** -0.5 * LOG2E, 1.0).astype(F32)

    (w_in_b, w_glu_b, w_gate_b, w_brd_b, w_brr_b, w_brs_b, w_o_b, w_fg_b, w_fu_b, w_fd_b, w_ple_b,
     w_pgd_b, w_pgu_b) = (w.astype(BF16) for w in (
         w_in, s5_w_glu, w_gate_up, w_br_diff, w_br_ret, w_br_s5, w_o, w_ffn_gate, w_ffn_up,
         w_ffn_down, w_ple, w_ple_gate_down, w_ple_gate_up))
    p3 = p.reshape(depth, S, p.shape[-1])

    for i in range(depth):
        lam_init = 0.8 - 0.6 * math.exp(-0.3 * i)
        z = _norm_mm(h, norm_mix[i], w_in_b, i, z_scale, bm=bm_in, bn=bn_in)
        o_diff = _diff_attention(z, rel_bias, bias_tiles, diff_lambda[i], diff_subln[i], lam_init,
                                 off_q=off_dq, off_k=off_dk, off_v=off_dv, n_heads=diff_heads, T=T_attn)
        o_ret = _retention(z, ret_tables, off_q=off_rq, off_k=off_rk, off_v=off_rv, off_g=off_rg,
                           n_heads=ret_heads, TR=T_ret)
        s5p = _s5_params(s5_lambda_re[i], s5_lambda_im[i], s5_log_dt[i], s5_b_re[i], s5_b_im[i],
                         s5_c_re[i], s5_c_im[i])
        o_s5 = _s5(z, s5p, s5_d[i], w_glu_b, i, off_u=off_su, width=s5_w, Tt=T_s5)
        mixed = _merge(z, o_diff, o_ret, o_s5, w_gate_b, w_brd_b, w_brr_b, w_brs_b, i,
                       off_gate=off_gate, bm=bm_mg, bn=bn_mg)
        h = _mm_res(mixed, w_o_b, i, h, bm=bm_o, bn=bn_o)
        act = _ffn_up(h, norm_ffn[i], w_fg_b, w_fu_b, i, bm=bm_up, bn=bn_up)
        h = _mm_res(act, w_fd_b, i, h, bm=bm_dn, bn=bn_dn)
        h = _ple(h, norm_ple[i], p3, w_ple_b, w_pgd_b, w_pgu_b, i, bm=bm_ple)
    out = _final_norm(h, norm_final, bm=_tile(S, 256))
    return out.reshape(B, S, D)
```

```python
import functools
import math

import jax
import jax.numpy as jnp
from jax import lax
from jax.experimental import pallas as pl
from jax.experimental.pallas import tpu as pltpu

F32 = jnp.float32
BF16 = jnp.bfloat16
EPS = 1e-6

HEAD_DIM = 128
CHUNK = 128
REL_BUCKETS = 32
REL_MAX_DIST = 128
S5_GROUP = 16
S5_STATE = 64
S5_SLICE_GROUPS = 8
MASK_VALUE = -1e30
LOG2E = math.log2(math.e)
ATTN_ROW_CHUNK = 128
ATTN_HEADS_PER_STEP = 2

V7X_VMEM_BYTES = 64 * 1024 * 1024
V7X_VMEM_REQUEST_CAP = 60 * 1024 * 1024
SPILL_AND_TEMP_ALLOWANCE = 8 * 1024 * 1024


def _cparams(n_axes, est_bytes):
    limit = int(min(est_bytes + SPILL_AND_TEMP_ALLOWANCE, V7X_VMEM_REQUEST_CAP))
    return pltpu.CompilerParams(dimension_semantics=("arbitrary",) * n_axes, vmem_limit_bytes=limit)


def _tile(n, want):
    t = min(n, want)
    while n % t:
        t //= 2
    return t


def _rms_rows(x_ref, g_ref, dst_ref, rows):
    n = x_ref.shape[0] // rows

    def body(i, c):
        r = pl.multiple_of(i * rows, rows)
        x = x_ref[pl.ds(r, rows), :]
        ms = jnp.mean(x * x, axis=-1, keepdims=True)
        dst_ref[pl.ds(r, rows), :] = (x * lax.rsqrt(ms + EPS) * g_ref[...]).astype(dst_ref.dtype)
        return c

    lax.fori_loop(0, n, body, 0)


def _lane_partial_sumsq(x):
    sq = x * x
    part = sq[:, 0:128]
    for j in range(1, x.shape[1] // 128):
        part = part + sq[:, j * 128:(j + 1) * 128]
    return part


def _row_rsqrt(ss_ref, n_cols):
    return lax.rsqrt(jnp.sum(ss_ref[...], axis=-1, keepdims=True) * (1.0 / n_cols) + EPS)


def _row_prep_kernel(x_ref, g_ref, hg_ref, ss_ref):
    x = x_ref[...]
    hg_ref[...] = (x * g_ref[...]).astype(hg_ref.dtype)
    ss_ref[...] = _lane_partial_sumsq(x)


def _row_prep(x, gain, *, bm):
    M, D = x.shape
    return pl.pallas_call(
        _row_prep_kernel,
        out_shape=(jax.ShapeDtypeStruct((M, D), BF16), jax.ShapeDtypeStruct((M, 128), F32)),
        grid=(M // bm,),
        in_specs=[pl.BlockSpec((bm, D), lambda i: (i, 0)), pl.BlockSpec((1, D), lambda i: (0, 0))],
        out_specs=(pl.BlockSpec((bm, D), lambda i: (i, 0)), pl.BlockSpec((bm, 128), lambda i: (i, 0))),
        compiler_params=_cparams(1, 2 * (bm * D * 6 + bm * 512) + 2 * bm * D * 4),
        name="row_prep",
    )(x, gain.reshape(1, D))


def _norm_mm_kernel(hg_ref, ss_ref, w_ref, cs_ref, o_ref):
    acc = jnp.dot(hg_ref[...], w_ref[...], preferred_element_type=F32)
    r = _row_rsqrt(ss_ref, hg_ref.shape[1])
    o_ref[...] = ((acc * r) * cs_ref[...]).astype(o_ref.dtype)


def _wspec(layer, k, bn, col_block=lambda j: j):
    return pl.BlockSpec((None, k, bn), lambda i, j: (layer, 0, col_block(j)))


def _norm_mm(hg, ss, w, layer, col_scale, *, bm, bn, out_dtype=BF16):
    M, K = hg.shape
    N = w.shape[2]
    est = 2 * (bm * K * 2 + bm * 512 + K * bn * 2 + bm * bn * 2) + 4 * bm * bn * 4
    return pl.pallas_call(
        _norm_mm_kernel,
        out_shape=jax.ShapeDtypeStruct((M, N), out_dtype),
        grid=(M // bm, N // bn),
        in_specs=[pl.BlockSpec((bm, K), lambda i, j: (i, 0)),
                  pl.BlockSpec((bm, 128), lambda i, j: (i, 0)),
                  _wspec(layer, K, bn),
                  pl.BlockSpec((1, bn), lambda i, j: (0, j))],
        out_specs=pl.BlockSpec((bm, bn), lambda i, j: (i, j)),
        compiler_params=_cparams(2, est),
        name="norm_mm",
    )(hg, ss, w, col_scale.reshape(1, N))


def _bias_tile_kernel(rb_ref, o_ref):
    h = pl.program_id(0)
    d = pl.program_id(1)
    T = o_ref.shape[2]
    r = lax.broadcasted_iota(jnp.int32, (T, T), 0)
    c = lax.broadcasted_iota(jnp.int32, (T, T), 1)
    rel = d * T + r - c
    n = jnp.maximum(rel, 0)
    max_exact = REL_BUCKETS // 2
    nf = jnp.maximum(n, 1).astype(F32)
    large = max_exact + (jnp.log(nf / max_exact) / math.log(REL_MAX_DIST / max_exact)
                         * (REL_BUCKETS - max_exact)).astype(jnp.int32)
    large = jnp.minimum(large, REL_BUCKETS - 1)
    bucket = jnp.where(n < max_exact, n, large)
    val = jnp.zeros((T, T), F32)
    for b in range(REL_BUCKETS):
        val = jnp.where(bucket == b, rb_ref[b, h], val)
    o_ref[0, 0] = jnp.where(rel >= 0, val * LOG2E, MASK_VALUE)


def _bias_tiles(rel_bias, n_heads, T):
    return pl.pallas_call(
        _bias_tile_kernel,
        out_shape=jax.ShapeDtypeStruct((n_heads, 2, T, T), F32),
        grid=(n_heads, 2),
        in_specs=[pl.BlockSpec(memory_space=pltpu.SMEM)],
        out_specs=pl.BlockSpec((1, 1, T, T), lambda h, d: (h, d, 0, 0)),
        compiler_params=_cparams(2, 12 * T * T * 4),
        name="t5_bias_tiles",
    )(rel_bias)


def _attn_kernel(qt_ref, kt_ref, rb_ref, li_ref, dl_ref, sub_ref, q_ref, k_ref, v_ref, bt_ref,
                 o_ref, m_ref, l_ref, acc_ref, *, row_chunks):
    hg = pl.program_id(0)
    p = pl.program_id(1)
    qi = qt_ref[p]
    ki = kt_ref[p]
    T = q_ref.shape[0]
    W = 2 * HEAD_DIM
    hb = q_ref.shape[1] // W

    @pl.when(ki == 0)
    def _():
        m_ref[...] = jnp.full(m_ref.shape, MASK_VALUE, F32)
        l_ref[...] = jnp.zeros(l_ref.shape, F32)
        acc_ref[...] = jnp.zeros(acc_ref.shape, F32)

    def step(tile, causal):
        R = T // row_chunks
        for r in range(row_chunks):
            rows = slice(r * R, (r + 1) * R)
            ncol = (r + 1) * R if causal else T
            for hh in range(hb):
                v = v_ref[0:ncol, hh * W:(hh + 1) * W]
                bias_const = None if tile is not None else rb_ref[REL_BUCKETS - 1, hg * hb + hh] * LOG2E
                for c in range(2):
                    pc = 2 * hh + c
                    col0 = hh * W + c * HEAD_DIM
                    qc = q_ref[rows, col0:col0 + HEAD_DIM]
                    kc = k_ref[0:ncol, col0:col0 + HEAD_DIM]
                    s = lax.dot_general(qc, kc, (((1,), (1,)), ((), ())), preferred_element_type=F32)
                    if tile is not None:
                        s = s + bt_ref[hh, tile, rows, 0:ncol]
                    m_prev = m_ref[pc, rows, :]
                    row_max = jnp.max(s, axis=-1, keepdims=True)
                    if bias_const is not None:
                        row_max = row_max + bias_const
                    m_new = jnp.maximum(m_prev, row_max)
                    alpha = jnp.exp2(m_prev - m_new)
                    shift = m_new if bias_const is None else m_new - bias_const
                    l_part = alpha * l_ref[pc, rows, :]
                    ps = []
                    for j in range(ncol // 128):
                        pj = jnp.exp2(s[:, j * 128:(j + 1) * 128] - shift)
                        l_part = l_part + pj
                        ps.append(pj.astype(BF16))
                    l_ref[pc, rows, :] = l_part
                    m_ref[pc, rows, :] = m_new
                    pv = jnp.dot(jnp.concatenate(ps, axis=1), v, preferred_element_type=F32)
                    acc_ref[pc, rows, :] = (jnp.concatenate([alpha] * (W // 128), axis=1)
                                            * acc_ref[pc, rows, :] + pv)

    @pl.when(qi - ki >= 2)
    def _():
        step(None, False)

    @pl.when(qi - ki == 1)
    def _():
        step(1, False)

    @pl.when(qi == ki)
    def _():
        step(0, True)
        lam_init = li_ref[0]
        dl = dl_ref[...]
        lam = (jnp.exp(jnp.sum(dl[0:1] * dl[1:2], axis=-1, keepdims=True))
               - jnp.exp(jnp.sum(dl[2:3] * dl[3:4], axis=-1, keepdims=True)) + lam_init)
        for hh in range(hb):
            l0 = jnp.sum(l_ref[2 * hh], axis=-1, keepdims=True)
            l1 = jnp.sum(l_ref[2 * hh + 1], axis=-1, keepdims=True)
            o = acc_ref[2 * hh] / l0 - lam * (acc_ref[2 * hh + 1] / l1)
            ms = jnp.mean(o * o, axis=-1, keepdims=True)
            y = (o * lax.rsqrt(ms + EPS) * sub_ref[...]) * (1.0 - lam_init)
            o_ref[:, hh * W:(hh + 1) * W] = y.astype(o_ref.dtype)


def _diff_attention(z, rel_bias, bias_tiles, diff_lambda, subln, lam_init, *, off_q, off_k, off_v,
                    n_heads, T):
    S = z.shape[0]
    W = 2 * HEAD_DIM
    nq = S // T
    qs, ks = [], []
    for qi in range(nq):
        for ki in range(qi + 1):
            qs.append(qi)
            ks.append(ki)
    qt = jnp.asarray(qs, jnp.int32)
    kt = jnp.asarray(ks, jnp.int32)
    hb = ATTN_HEADS_PER_STEP if n_heads % ATTN_HEADS_PER_STEP == 0 else 1
    WB = hb * W
    assert off_q % WB == 0 and off_k % WB == 0 and off_v % WB == 0
    bq, bk, bv = off_q // WB, off_k // WB, off_v // WB
    grid_spec = pltpu.PrefetchScalarGridSpec(
        num_scalar_prefetch=2,
        grid=(n_heads // hb, len(qs)),
        in_specs=[pl.BlockSpec(memory_space=pltpu.SMEM),
                  pl.BlockSpec(memory_space=pltpu.SMEM),
                  pl.BlockSpec((4, HEAD_DIM), lambda h, p, qt, kt: (0, 0)),
                  pl.BlockSpec((1, W), lambda h, p, qt, kt: (0, 0)),
                  pl.BlockSpec((T, WB), lambda h, p, qt, kt: (qt[p], bq + h)),
                  pl.BlockSpec((T, WB), lambda h, p, qt, kt: (kt[p], bk + h)),
                  pl.BlockSpec((T, WB), lambda h, p, qt, kt: (kt[p], bv + h)),
                  pl.BlockSpec((hb, 2, T, T), lambda h, p, qt, kt: (h, 0, 0, 0),
                               pipeline_mode=pl.Buffered(1))],
        out_specs=pl.BlockSpec((T, WB), lambda h, p, qt, kt: (qt[p], h)),
        scratch_shapes=[pltpu.VMEM((2 * hb, T, 128), F32), pltpu.VMEM((2 * hb, T, 128), F32),
                        pltpu.VMEM((2 * hb, T, W), F32)],
    )
    est = (2 * 4 * T * WB * 2 + hb * 2 * T * T * 4 + 2 * hb * T * W * 4 + 4 * hb * T * 128 * 4
           + 16 * ATTN_ROW_CHUNK * T * 4)
    return pl.pallas_call(
        functools.partial(_attn_kernel, row_chunks=max(1, T // ATTN_ROW_CHUNK)),
        out_shape=jax.ShapeDtypeStruct((S, n_heads * W), BF16),
        grid_spec=grid_spec,
        compiler_params=_cparams(2, est),
        name="diff_attention",
    )(qt, kt, rel_bias, jnp.full((1,), lam_init, F32), diff_lambda, subln.reshape(1, W),
      z, z, z, bias_tiles)


def _ret_kernel(cd_ref, q_ref, k_ref, v_ref, g_ref, cos_ref, sin_ref, dm_ref, xi_ref, ze_ref,
                o_ref, R_ref):
    t = pl.program_id(0)
    n_heads = R_ref.shape[0]

    @pl.when(t == 0)
    def _():
        R_ref[...] = jnp.zeros(R_ref.shape, F32)

    half = HEAD_DIM // 2
    Rs = [R_ref[h] for h in range(n_heads)]
    for c in range(q_ref.shape[0] // CHUNK):
        sl = slice(c * CHUNK, (c + 1) * CHUNK)
        cs = cos_ref[sl, :]
        sn = sin_ref[sl, :]
        for h in range(n_heads):
            hc = slice(h * HEAD_DIM, (h + 1) * HEAD_DIM)
            q = q_ref[sl, hc].astype(F32)
            k = k_ref[sl, hc].astype(F32)
            v = v_ref[sl, hc]
            qr = q * cs + pltpu.roll(q, half, 1) * sn
            kr = (k * cs + pltpu.roll(k, half, 1) * sn) * (HEAD_DIM ** -0.5)
            qb = qr.astype(BF16)
            inner = lax.dot_general(qb, kr.astype(BF16), (((1,), (1,)), ((), ())),
                                    preferred_element_type=F32) * dm_ref[h]
            o = (jnp.dot(inner.astype(BF16), v, preferred_element_type=F32)
                 + jnp.dot(qb, Rs[h].astype(BF16), preferred_element_type=F32) * xi_ref[h])
            Rs[h] = Rs[h] * cd_ref[h] + lax.dot_general(
                (kr * ze_ref[h]).astype(BF16), v, (((0,), (0,)), ((), ())), preferred_element_type=F32)
            o = o * lax.rsqrt(jnp.mean(o * o, axis=-1, keepdims=True) + EPS)
            g = g_ref[sl, hc].astype(F32)
            o_ref[sl, hc] = (jax.nn.silu(g) * o).astype(o_ref.dtype)
    for h in range(n_heads):
        R_ref[h] = Rs[h]


def _retention_tables(S, n_heads):
    half = HEAD_DIM // 2
    pos = jnp.arange(S, dtype=F32)
    theta = 1.0 / (10000.0 ** jnp.linspace(0.0, 1.0, half, dtype=F32))
    ang = pos[:, None] * theta[None, :]
    cos = jnp.cos(ang)
    sin = jnp.sin(ang)
    cosf = jnp.concatenate([cos, cos], axis=-1)
    sinf = jnp.concatenate([-sin, sin], axis=-1)
    log_gamma = jnp.log1p(-jnp.exp2(-5.0 - jnp.arange(n_heads, dtype=F32)))
    idx = jnp.arange(CHUNK)
    diff = idx[:, None] - idx[None, :]
    dmat = jnp.where(diff >= 0,
                     jnp.exp(log_gamma[:, None, None] * jnp.maximum(diff, 0).astype(F32)), 0.0)
    xi = jnp.exp(log_gamma[:, None] * (idx[None, :] + 1).astype(F32))[..., None]
    zeta = jnp.exp(log_gamma[:, None] * (CHUNK - 1 - idx)[None, :].astype(F32))[..., None]
    chunk_decay = jnp.exp(log_gamma * CHUNK)
    return cosf, sinf, dmat, xi, zeta, chunk_decay


def _retention(z, tables, *, off_q, off_k, off_v, off_g, n_heads, TR):
    S = z.shape[0]
    cosf, sinf, dmat, xi, zeta, chunk_decay = tables
    WR = n_heads * HEAD_DIM
    assert all(o % WR == 0 for o in (off_q, off_k, off_v, off_g))
    blk = lambda off: pl.BlockSpec((TR, WR), lambda t: (t, off // WR))
    tab = pl.BlockSpec((TR, HEAD_DIM), lambda t: (t, 0))
    whole = lambda last: pl.BlockSpec((n_heads, CHUNK, last), lambda t: (0, 0, 0))
    est = (2 * (5 * TR * WR * 2 + 2 * TR * HEAD_DIM * 4 + n_heads * 3 * CHUNK * CHUNK * 4)
           + n_heads * 16 * CHUNK * CHUNK * 4)
    return pl.pallas_call(
        _ret_kernel,
        out_shape=jax.ShapeDtypeStruct((S, WR), BF16),
        grid=(S // TR,),
        in_specs=[pl.BlockSpec(memory_space=pltpu.SMEM),
                  blk(off_q), blk(off_k), blk(off_v), blk(off_g), tab, tab,
                  whole(CHUNK), whole(1), whole(1)],
        out_specs=pl.BlockSpec((TR, WR), lambda t: (t, 0)),
        scratch_shapes=[pltpu.VMEM((n_heads, HEAD_DIM, HEAD_DIM), F32)],
        compiler_params=_cparams(1, est),
        name="retention",
    )(chunk_decay, z, z, z, z, cosf, sinf, dmat, xi, zeta)


def _s5_kernel(u_ref, benc_ref, apr_ref, api_ref, cdec_ref, d_ref, wglu_ref, o_ref,
               carry_ref, z_ref):
    t = pl.program_id(0)
    j = pl.program_id(1)
    nsl = pl.num_programs(1)
    Tt = u_ref.shape[0]
    C = apr_ref.shape[2]

    @pl.when((t == 0) & (j == 0))
    def _():
        carry_ref[...] = jnp.zeros(carry_ref.shape, F32)

    u = u_ref[...]
    x = jnp.dot(u, benc_ref[0], preferred_element_type=F32)
    ng = Tt // 8
    hr = x[:, :C].reshape(ng, 8, C)
    hi = x[:, C:].reshape(ng, 8, C)
    apr = apr_ref[0]
    api = api_ref[0]
    sub = lax.broadcasted_iota(jnp.int32, (ng, 8, C), 1)
    for d in (1, 2, 4):
        a_r = apr[d - 1:d, :].reshape(1, 1, C)
        a_i = api[d - 1:d, :].reshape(1, 1, C)
        sr = pltpu.roll(hr, d, 1)
        si = pltpu.roll(hi, d, 1)
        m = sub >= d
        nr = hr + jnp.where(m, a_r * sr - a_i * si, 0.0)
        ni = hi + jnp.where(m, a_r * si + a_i * sr, 0.0)
        hr, hi = nr, ni
    cr = carry_ref[j]
    c_r = cr[:, :C]
    c_i = cr[:, C:]
    out_r, out_i = [], []
    for g in range(ng):
        b_r = jnp.broadcast_to(c_r, (8, C))
        b_i = jnp.broadcast_to(c_i, (8, C))
        g_r = hr[g] + (apr * b_r - api * b_i)
        g_i = hi[g] + (apr * b_i + api * b_r)
        out_r.append(g_r)
        out_i.append(g_i)
        c_r = g_r[7:8, :]
        c_i = g_i[7:8, :]
    carry_ref[j] = jnp.concatenate([c_r, c_i], axis=1)
    hcat = jnp.concatenate([jnp.concatenate(out_r, axis=0), jnp.concatenate(out_i, axis=0)],
                           axis=1).astype(BF16)
    y = jnp.dot(hcat, cdec_ref[0], preferred_element_type=F32) + d_ref[0] * u.astype(F32)
    z_ref[j] = jax.nn.gelu(y).astype(BF16)

    @pl.when(j == nsl - 1)
    def _():
        zc = jnp.concatenate([z_ref[s] for s in range(z_ref.shape[0])], axis=1)
        zg = jnp.dot(zc, wglu_ref[...], preferred_element_type=F32)
        W = zg.shape[1] // 2
        o_ref[...] = (zg[:, :W] * jax.nn.sigmoid(zg[:, W:])).astype(o_ref.dtype)


def _s5_params(lam_re, lam_im, log_dt, b_re, b_im, c_re, c_im):
    G, N = lam_re.shape
    sg = S5_SLICE_GROUPS
    nsl = G // sg
    lr, li = lam_re.astype(F32), lam_im.astype(F32)
    dt = jnp.exp(log_dt.astype(F32))[:, None]
    mag = jnp.exp(lr * dt)
    ar = mag * jnp.cos(li * dt)
    ai = mag * jnp.sin(li * dt)
    den = lr * lr + li * li
    fr = ((ar - 1.0) * lr + ai * li) / den
    fi = (ai * lr - (ar - 1.0) * li) / den
    br_, bi_ = b_re.astype(F32), b_im.astype(F32)
    bbr = fr[..., None] * br_ - fi[..., None] * bi_
    bbi = fr[..., None] * bi_ + fi[..., None] * br_
    eye = jnp.eye(sg, dtype=F32)

    def enc(bb):
        return jnp.einsum('sgnj,gh->sgjhn', bb.reshape(nsl, sg, N, S5_GROUP), eye).reshape(
            nsl, sg * S5_GROUP, sg * N)

    def dec(cc):
        return jnp.einsum('sgjn,gh->sgnhj', cc.reshape(nsl, sg, S5_GROUP, N), eye).reshape(
            nsl, sg * N, sg * S5_GROUP)

    benc = jnp.concatenate([enc(bbr), enc(bbi)], axis=-1).astype(BF16)
    cdec = jnp.concatenate([dec(c_re.astype(F32)), -dec(c_im.astype(F32))], axis=1).astype(BF16)
    prs, pis = [ar], [ai]
    for _ in range(7):
        pr, pi = prs[-1], pis[-1]
        prs.append(pr * ar - pi * ai)
        pis.append(pr * ai + pi * ar)
    apr = jnp.stack(prs).reshape(8, nsl, sg * N).transpose(1, 0, 2)
    api = jnp.stack(pis).reshape(8, nsl, sg * N).transpose(1, 0, 2)
    return benc, cdec, apr, api


def _s5(z, s5p, d_skip, wglu, layer, *, off_u, width, Tt):
    S = z.shape[0]
    benc, cdec, apr, api = s5p
    nsl, uw, xw = benc.shape
    C = xw // 2
    lp = apr.shape[1]
    bu = off_u // uw
    est = (2 * (Tt * uw * 2 + uw * xw * 2 + 2 * lp * C * 4 + xw * uw * 2 + width * 2 * width * 2
                + Tt * width * 2) + nsl * Tt * uw * 2 + 24 * Tt * C * 4 + Tt * 2 * width * 4 * 2)
    return pl.pallas_call(
        _s5_kernel,
        out_shape=jax.ShapeDtypeStruct((S, width), BF16),
        grid=(S // Tt, nsl),
        in_specs=[pl.BlockSpec((Tt, uw), lambda t, j: (t, bu + j)),
                  pl.BlockSpec((1, uw, xw), lambda t, j: (j, 0, 0)),
                  pl.BlockSpec((1, lp, C), lambda t, j: (j, 0, 0)),
                  pl.BlockSpec((1, lp, C), lambda t, j: (j, 0, 0)),
                  pl.BlockSpec((1, xw, uw), lambda t, j: (j, 0, 0)),
                  pl.BlockSpec((1, 1, uw), lambda t, j: (j, 0, 0)),
                  pl.BlockSpec((None, width, 2 * width), lambda t, j: (layer, 0, 0))],
        out_specs=pl.BlockSpec((Tt, width), lambda t, j: (t, 0)),
        scratch_shapes=[pltpu.VMEM((nsl, 1, xw), F32), pltpu.VMEM((nsl, Tt, uw), BF16)],
        compiler_params=_cparams(2, est),
        name="s5",
    )(z, benc, apr, api, cdec, d_skip.astype(F32).reshape(nsl, 1, uw), wglu)


def _merge_kernel(gl_ref, od_ref, or_ref, os_ref, wg0_ref, wg1_ref, wg2_ref, wd_ref, wr_ref, ws_ref,
                  o_ref):
    gl = gl_ref[...]

    def branch(wg_ref, x_ref, w_ref):
        gate = jax.nn.sigmoid(jnp.dot(gl, wg_ref[...], preferred_element_type=F32))
        return gate * jnp.dot(x_ref[...], w_ref[...], preferred_element_type=F32)

    o_ref[...] = (branch(wg0_ref, od_ref, wd_ref) + branch(wg1_ref, or_ref, wr_ref)
                  + branch(wg2_ref, os_ref, ws_ref)).astype(o_ref.dtype)


def _merge(z, o_diff, o_ret, o_s5, wg, wd, wr, ws, layer, *, off_gate, bm, bn):
    S = z.shape[0]
    R = wg.shape[1]
    D = wd.shape[2]
    nb = D // bn
    kd, kr, ks = wd.shape[1], wr.shape[1], ws.shape[1]
    row = lambda k: pl.BlockSpec((bm, k), lambda i, j: (i, 0))
    col = lambda k: _wspec(layer, k, bn)
    est = 2 * (bm * (R + kd + kr + ks) * 2 + (3 * R + kd + kr + ks) * bn * 2 + bm * bn * 2) + 8 * bm * bn * 4
    return pl.pallas_call(
        _merge_kernel,
        out_shape=jax.ShapeDtypeStruct((S, D), BF16),
        grid=(S // bm, nb),
        in_specs=[pl.BlockSpec((bm, R), lambda i, j: (i, off_gate // R)),
                  row(kd), row(kr), row(ks),
                  _wspec(layer, R, bn),
                  _wspec(layer, R, bn, lambda j: nb + j),
                  _wspec(layer, R, bn, lambda j: 2 * nb + j),
                  col(kd), col(kr), col(ks)],
        out_specs=pl.BlockSpec((bm, bn), lambda i, j: (i, j)),
        compiler_params=_cparams(2, est),
        name="gated_merge",
    )(z, o_diff, o_ret, o_s5, wg, wg, wg, wd, wr, ws)


def _mm_res_kernel(x_ref, w_ref, h_ref, o_ref):
    o_ref[...] = h_ref[...] + jnp.dot(x_ref[...], w_ref[...], preferred_element_type=F32)


def _mm_res_norm_kernel(x_ref, w_ref, h_ref, g_ref, o_ref, hg_ref, ss_ref):
    hn = h_ref[...] + jnp.dot(x_ref[...], w_ref[...], preferred_element_type=F32)
    o_ref[...] = hn
    hg_ref[...] = (hn * g_ref[...]).astype(hg_ref.dtype)
    part = _lane_partial_sumsq(hn)

    @pl.when(pl.program_id(1) == 0)
    def _():
        ss_ref[...] = part

    @pl.when(pl.program_id(1) > 0)
    def _():
        ss_ref[...] += part


def _mm_res(x, w, layer, h, *, bm, bn, in_place, next_gain=None):
    M, K = x.shape
    N = w.shape[2]
    est = 2 * (bm * K * 2 + K * bn * 2 + 2 * bm * bn * 4 + bm * bn * 2 + bm * 512) + 2 * bm * bn * 4
    in_specs = [pl.BlockSpec((bm, K), lambda i, j: (i, 0)),
                _wspec(layer, K, bn),
                pl.BlockSpec((bm, bn), lambda i, j: (i, j))]
    h_spec = pl.BlockSpec((bm, bn), lambda i, j: (i, j))
    common = dict(grid=(M // bm, N // bn), input_output_aliases={2: 0} if in_place else {},
                  compiler_params=_cparams(2, est))
    if next_gain is None:
        return pl.pallas_call(
            _mm_res_kernel, out_shape=jax.ShapeDtypeStruct((M, N), F32), in_specs=in_specs,
            out_specs=h_spec, name="mm_residual", **common)(x, w, h)
    return pl.pallas_call(
        _mm_res_norm_kernel,
        out_shape=(jax.ShapeDtypeStruct((M, N), F32), jax.ShapeDtypeStruct((M, N), BF16),
                   jax.ShapeDtypeStruct((M, 128), F32)),
        in_specs=in_specs + [pl.BlockSpec((1, bn), lambda i, j: (0, j))],
        out_specs=(h_spec, pl.BlockSpec((bm, bn), lambda i, j: (i, j)),
                   pl.BlockSpec((bm, 128), lambda i, j: (i, 0))),
        name="mm_residual_norm", **common)(x, w, h, next_gain.reshape(1, N))


def _ffn_up_kernel(hg_ref, ss_ref, wg_ref, wu_ref, o_ref):
    hg = hg_ref[...]
    r = _row_rsqrt(ss_ref, hg_ref.shape[1])
    a = jnp.dot(hg, wg_ref[...], preferred_element_type=F32) * r
    b = jnp.dot(hg, wu_ref[...], preferred_element_type=F32) * r
    o_ref[...] = (jax.nn.silu(a) * b).astype(o_ref.dtype)


def _ffn_up(hg, ss, wg, wu, layer, *, bm, bn):
    M, K = hg.shape
    N = wg.shape[2]
    est = 2 * (bm * K * 2 + bm * 512 + 2 * K * bn * 2 + bm * bn * 2) + 5 * bm * bn * 4
    return pl.pallas_call(
        _ffn_up_kernel,
        out_shape=jax.ShapeDtypeStruct((M, N), BF16),
        grid=(M // bm, pl.cdiv(N, bn)),
        in_specs=[pl.BlockSpec((bm, K), lambda i, j: (i, 0)),
                  pl.BlockSpec((bm, 128), lambda i, j: (i, 0)),
                  _wspec(layer, K, bn),
                  _wspec(layer, K, bn)],
        out_specs=pl.BlockSpec((bm, bn), lambda i, j: (i, j)),
        compiler_params=_cparams(2, est),
        name="ffn_up",
    )(hg, ss, wg, wu)


def _ple_update(h_ref, g_ref, p_ref, wple_ref, wgd_ref, wgu_ref, hp_ref):
    _rms_rows(h_ref, g_ref, hp_ref, min(64, h_ref.shape[0]))
    t = jnp.dot(hp_ref[...], wgd_ref[...], preferred_element_type=F32).astype(BF16)
    gate = jax.nn.sigmoid(jnp.dot(t, wgu_ref[...], preferred_element_type=F32))
    e = jnp.dot(p_ref[...].astype(BF16), wple_ref[...], preferred_element_type=F32)
    return h_ref[...] + e * gate


def _ple_kernel(h_ref, g_ref, p_ref, wple_ref, wgd_ref, wgu_ref, gn_ref, o_ref, hg_ref, ss_ref, hp_ref):
    hn = _ple_update(h_ref, g_ref, p_ref, wple_ref, wgd_ref, wgu_ref, hp_ref)
    o_ref[...] = hn
    hg_ref[...] = (hn * gn_ref[...]).astype(hg_ref.dtype)
    ss_ref[...] = _lane_partial_sumsq(hn)


def _ple_final_kernel(h_ref, g_ref, p_ref, wple_ref, wgd_ref, wgu_ref, gn_ref, o_ref, hp_ref):
    hn = _ple_update(h_ref, g_ref, p_ref, wple_ref, wgd_ref, wgu_ref, hp_ref)
    ms = jnp.mean(hn * hn, axis=-1, keepdims=True)
    o_ref[...] = hn * lax.rsqrt(ms + EPS) * gn_ref[...]


def _ple(h, gain, p, wple, wgd, wgu, layer, next_gain, *, bm, final):
    M, D = h.shape
    P = p.shape[2]
    R = wgd.shape[2]
    est = 2 * (2 * bm * D * 4 + bm * D * 2 + bm * P * 4 + (P + 2 * R) * D * 2) + bm * D * 2 + 4 * bm * D * 4
    row = pl.BlockSpec((bm, D), lambda i: (i, 0))
    in_specs = [row,
                pl.BlockSpec((1, D), lambda i: (0, 0)),
                pl.BlockSpec((None, bm, P), lambda i: (layer, i, 0)),
                pl.BlockSpec((None, P, D), lambda i: (layer, 0, 0)),
                pl.BlockSpec((None, D, R), lambda i: (layer, 0, 0)),
                pl.BlockSpec((None, R, D), lambda i: (layer, 0, 0)),
                pl.BlockSpec((1, D), lambda i: (0, 0))]
    common = dict(grid=(M // bm,), in_specs=in_specs, scratch_shapes=[pltpu.VMEM((bm, D), BF16)],
                  input_output_aliases={0: 0}, compiler_params=_cparams(1, est))
    args = (h, gain.reshape(1, D), p, wple, wgd, wgu, next_gain.reshape(1, D))
    if final:
        return pl.pallas_call(_ple_final_kernel, out_shape=jax.ShapeDtypeStruct((M, D), F32),
                              out_specs=row, name="ple_final", **common)(*args)
    return pl.pallas_call(
        _ple_kernel,
        out_shape=(jax.ShapeDtypeStruct((M, D), F32), jax.ShapeDtypeStruct((M, D), BF16),
                   jax.ShapeDtypeStruct((M, 128), F32)),
        out_specs=(row, row, pl.BlockSpec((bm, 128), lambda i: (i, 0))),
        name="ple", **common)(*args)


def kernel(x, p, rel_bias, norm_mix, w_in, diff_lambda, diff_subln, s5_lambda_re, s5_lambda_im, s5_log_dt, s5_b_re, s5_b_im, s5_c_re, s5_c_im, s5_d, s5_w_glu, w_gate_up, w_br_diff, w_br_ret, w_br_s5, w_o, norm_ffn, w_ffn_gate, w_ffn_up, w_ffn_down, norm_ple, w_ple, w_ple_gate_down, w_ple_gate_up, norm_final):
    B, S, D = x.shape
    assert B == 1
    depth = w_in.shape[0]
    diff_w, ret_w, s5_w = w_br_diff.shape[1], w_br_ret.shape[1], w_br_s5.shape[1]
    gate_rank = w_gate_up.shape[1]
    diff_heads = diff_w // (2 * HEAD_DIM)
    ret_heads = ret_w // HEAD_DIM
    off_dq = 0
    off_dk = off_dq + diff_w
    off_dv = off_dk + diff_w
    off_rq = off_dv + diff_w
    off_rk = off_rq + ret_w
    off_rv = off_rk + ret_w
    off_rg = off_rv + ret_w
    off_su = off_rg + ret_w
    off_gate = off_su + s5_w
    in_w = off_gate + gate_rank
    assert w_in.shape[2] == in_w
    hidden = w_ffn_gate.shape[2]

    T_attn = _tile(S, 1024)
    T_ret = _tile(S, 256)
    T_s5 = _tile(S, 512)
    bm_in = _tile(S, 1024)
    bn_in = _tile(in_w, 768) if in_w % 768 == 0 else _tile(in_w, 512)
    bm_mg, bn_mg = _tile(S, 1024), _tile(D, 512)
    bm_o, bn_o = _tile(S, 1024), _tile(D, 512)
    bm_up, bn_up = _tile(S, 1024), min(hidden, 512)
    bm_dn, bn_dn = _tile(S, 512), _tile(D, 256)
    bm_ple = _tile(S, 256)

    h = x.reshape(S, D)
    hg, ss = _row_prep(h, norm_mix[0], bm=bm_ple)
    bias_tiles = _bias_tiles(rel_bias, diff_heads, T_attn)
    ret_tables = _retention_tables(S, ret_heads)
    z_scale = jnp.where(jnp.arange(in_w) < off_dk, HEAD_DIM ** -0.5 * LOG2E, 1.0).astype(F32)

    (w_in_b, w_glu_b, w_gate_b, w_brd_b, w_brr_b, w_brs_b, w_o_b, w_fg_b, w_fu_b, w_fd_b, w_ple_b,
     w_pgd_b, w_pgu_b) = (w.astype(BF16) for w in (
         w_in, s5_w_glu, w_gate_up, w_br_diff, w_br_ret, w_br_s5, w_o, w_ffn_gate, w_ffn_up,
         w_ffn_down, w_ple, w_ple_gate_down, w_ple_gate_up))
    p3 = p.reshape(depth, S, p.shape[-1])

    for i in range(depth):
        lam_init = 0.8 - 0.6 * math.exp(-0.3 * i)
        z = _norm_mm(hg, ss, w_in_b, i, z_scale, bm=bm_in, bn=bn_in)
        o_diff = _diff_attention(z, rel_bias, bias_tiles, diff_lambda[i], diff_subln[i], lam_init,
                                 off_q=off_dq, off_k=off_dk, off_v=off_dv, n_heads=diff_heads, T=T_attn)
        o_ret = _retention(z, ret_tables, off_q=off_rq, off_k=off_rk, off_v=off_rv, off_g=off_rg,
                           n_heads=ret_heads, TR=T_ret)
        s5p = _s5_params(s5_lambda_re[i], s5_lambda_im[i], s5_log_dt[i], s5_b_re[i], s5_b_im[i],
                         s5_c_re[i], s5_c_im[i])
        o_s5 = _s5(z, s5p, s5_d[i], w_glu_b, i, off_u=off_su, width=s5_w, Tt=T_s5)
        mixed = _merge(z, o_diff, o_ret, o_s5, w_gate_b, w_brd_b, w_brr_b, w_brs_b, i,
                       off_gate=off_gate, bm=bm_mg, bn=bn_mg)
        h, hg, ss = _mm_res(mixed, w_o_b, i, h, bm=bm_o, bn=bn_o, in_place=i > 0, next_gain=norm_ffn[i])
        act = _ffn_up(hg, ss, w_fg_b, w_fu_b, i, bm=bm_up, bn=bn_up)
        h = _mm_res(act, w_fd_b, i, h, bm=bm_dn, bn=bn_dn, in_place=True)
        if i + 1 < depth:
            h, hg, ss = _ple(h, norm_ple[i], p3, w_ple_b, w_pgd_b, w_pgu_b, i, norm_mix[i + 1],
                             bm=bm_ple, final=False)
        else:
            out = _ple(h, norm_ple[i], p3, w_ple_b, w_pgd_b, w_pgu_b, i, norm_final,
                       bm=bm_ple, final=True)
    return out.reshape(B, S, D)
```

```python
import functools
import math

import jax
import jax.numpy as jnp
from jax import lax
from jax.experimental import pallas as pl
from jax.experimental.pallas import tpu as pltpu

F32 = jnp.float32
BF16 = jnp.bfloat16
EPS = 1e-6

HEAD_DIM = 128
CHUNK = 128
REL_BUCKETS = 32
REL_MAX_DIST = 128
S5_GROUP = 16
S5_STATE = 64
S5_SLICE_GROUPS = 8
S5_CHUNK = 16
MASK_VALUE = -1e30
LOG2E = math.log2(math.e)
ATTN_ROW_CHUNK = 128
ATTN_HEADS_PER_STEP = 2

V7X_VMEM_BYTES = 64 * 1024 * 1024
V7X_VMEM_REQUEST_CAP = 60 * 1024 * 1024
SPILL_AND_TEMP_ALLOWANCE = 8 * 1024 * 1024


def _cparams(n_axes, est_bytes):
    limit = int(min(est_bytes + SPILL_AND_TEMP_ALLOWANCE, V7X_VMEM_REQUEST_CAP))
    return pltpu.CompilerParams(dimension_semantics=("arbitrary",) * n_axes, vmem_limit_bytes=limit)


def _tile(n, want):
    t = min(n, want)
    while n % t:
        t //= 2
    return t


def _rms_rows(x_ref, g_ref, dst_ref, rows):
    n = x_ref.shape[0] // rows

    def body(i, c):
        r = pl.multiple_of(i * rows, rows)
        x = x_ref[pl.ds(r, rows), :]
        ms = jnp.mean(x * x, axis=-1, keepdims=True)
        dst_ref[pl.ds(r, rows), :] = (x * lax.rsqrt(ms + EPS) * g_ref[...]).astype(dst_ref.dtype)
        return c

    lax.fori_loop(0, n, body, 0)


def _lane_partial_sumsq(x):
    sq = x * x
    part = sq[:, 0:128]
    for j in range(1, x.shape[1] // 128):
        part = part + sq[:, j * 128:(j + 1) * 128]
    return part


def _row_rsqrt(ss_ref, n_cols):
    return lax.rsqrt(jnp.sum(ss_ref[...], axis=-1, keepdims=True) * (1.0 / n_cols) + EPS)


def _row_prep_kernel(x_ref, g_ref, hg_ref, ss_ref):
    x = x_ref[...]
    hg_ref[...] = (x * g_ref[...]).astype(hg_ref.dtype)
    ss_ref[...] = _lane_partial_sumsq(x)


def _row_prep(x, gain, *, bm):
    M, D = x.shape
    return pl.pallas_call(
        _row_prep_kernel,
        out_shape=(jax.ShapeDtypeStruct((M, D), BF16), jax.ShapeDtypeStruct((M, 128), F32)),
        grid=(M // bm,),
        in_specs=[pl.BlockSpec((bm, D), lambda i: (i, 0)), pl.BlockSpec((1, D), lambda i: (0, 0))],
        out_specs=(pl.BlockSpec((bm, D), lambda i: (i, 0)), pl.BlockSpec((bm, 128), lambda i: (i, 0))),
        compiler_params=_cparams(1, 2 * (bm * D * 6 + bm * 512) + 2 * bm * D * 4),
        name="row_prep",
    )(x, gain.reshape(1, D))


def _norm_mm_kernel(hg_ref, ss_ref, w_ref, cs_ref, o_ref):
    acc = jnp.dot(hg_ref[...], w_ref[...], preferred_element_type=F32)
    r = _row_rsqrt(ss_ref, hg_ref.shape[1])
    o_ref[...] = ((acc * r) * cs_ref[...]).astype(o_ref.dtype)


def _wspec(layer, k, bn, col_block=lambda j: j):
    return pl.BlockSpec((None, k, bn), lambda i, j: (layer, 0, col_block(j)))


def _norm_mm(hg, ss, w, layer, col_scale, *, bm, bn, out_dtype=BF16):
    M, K = hg.shape
    N = w.shape[2]
    est = 2 * (bm * K * 2 + bm * 512 + K * bn * 2 + bm * bn * 2) + 4 * bm * bn * 4
    return pl.pallas_call(
        _norm_mm_kernel,
        out_shape=jax.ShapeDtypeStruct((M, N), out_dtype),
        grid=(M // bm, N // bn),
        in_specs=[pl.BlockSpec((bm, K), lambda i, j: (i, 0)),
                  pl.BlockSpec((bm, 128), lambda i, j: (i, 0)),
                  _wspec(layer, K, bn),
                  pl.BlockSpec((1, bn), lambda i, j: (0, j))],
        out_specs=pl.BlockSpec((bm, bn), lambda i, j: (i, j)),
        compiler_params=_cparams(2, est),
        name="norm_mm",
    )(hg, ss, w, col_scale.reshape(1, N))


def _bias_band_kernel(rb_ref, o_ref):
    h = pl.program_id(0)
    R = o_ref.shape[1]
    r = lax.broadcasted_iota(jnp.int32, (R, 2 * R), 0)
    c = lax.broadcasted_iota(jnp.int32, (R, 2 * R), 1)
    rel = R + r - c
    n = jnp.maximum(rel, 0)
    max_exact = REL_BUCKETS // 2
    nf = jnp.maximum(n, 1).astype(F32)
    large = max_exact + (jnp.log(nf / max_exact) / math.log(REL_MAX_DIST / max_exact)
                         * (REL_BUCKETS - max_exact)).astype(jnp.int32)
    large = jnp.minimum(large, REL_BUCKETS - 1)
    bucket = jnp.where(n < max_exact, n, large)
    val = jnp.zeros((R, 2 * R), F32)
    for b in range(REL_BUCKETS):
        val = jnp.where(bucket == b, rb_ref[b, h], val)
    far = rb_ref[REL_BUCKETS - 1, h]
    o_ref[0] = jnp.where(rel >= 0, (val - far) * LOG2E, MASK_VALUE)


def _bias_band(rel_bias, n_heads, R):
    assert R >= REL_MAX_DIST
    return pl.pallas_call(
        _bias_band_kernel,
        out_shape=jax.ShapeDtypeStruct((n_heads, R, 2 * R), F32),
        grid=(n_heads,),
        in_specs=[pl.BlockSpec(memory_space=pltpu.SMEM)],
        out_specs=pl.BlockSpec((1, R, 2 * R), lambda h: (h, 0, 0)),
        compiler_params=_cparams(1, 24 * R * R * 4),
        name="t5_bias_band",
    )(rel_bias)


def _attn_kernel(qt_ref, kt_ref, rb_ref, li_ref, dl_ref, sub_ref, q_ref, k_ref, v_ref, bt_ref,
                 o_ref, m_ref, l_ref, acc_ref, *, row_chunks):
    hg = pl.program_id(0)
    p = pl.program_id(1)
    qi = qt_ref[p]
    ki = kt_ref[p]
    T = q_ref.shape[0]
    W = 2 * HEAD_DIM
    hb = q_ref.shape[1] // W

    @pl.when(ki == 0)
    def _():
        m_ref[...] = jnp.full(m_ref.shape, MASK_VALUE, F32)
        l_ref[...] = jnp.zeros(l_ref.shape, F32)
        acc_ref[...] = jnp.zeros(acc_ref.shape, F32)

    R = T // row_chunks
    nb = R // 128

    def step(kind):
        for r in range(row_chunks):
            rows = slice(r * R, (r + 1) * R)
            ncol = (r + 1) * R if kind == "diag" else T
            ncb = ncol // 128
            if kind == "diag":
                band0 = ncb - 2 * nb
            elif kind == "near" and r == 0:
                band0 = ncb - nb
            else:
                band0 = None
            for hh in range(hb):
                v = v_ref[0:ncol, hh * W:(hh + 1) * W]
                far = rb_ref[REL_BUCKETS - 1, hg * hb + hh] * LOG2E
                for c in range(2):
                    pc = 2 * hh + c
                    col0 = hh * W + c * HEAD_DIM
                    qc = q_ref[rows, col0:col0 + HEAD_DIM]
                    kc = k_ref[0:ncol, col0:col0 + HEAD_DIM]
                    s = lax.dot_general(qc, kc, (((1,), (1,)), ((), ())), preferred_element_type=F32)
                    blocks = []
                    for j in range(ncb):
                        sj = s[:, j * 128:(j + 1) * 128]
                        if band0 is not None and j - band0 >= 0:
                            jb = j - band0
                            sj = sj + bt_ref[hh, :, jb * 128:(jb + 1) * 128]
                        blocks.append(sj)
                    blk_max = blocks[0]
                    for sj in blocks[1:]:
                        blk_max = jnp.maximum(blk_max, sj)
                    m_prev = m_ref[pc, rows, :]
                    m_new = jnp.maximum(m_prev, jnp.max(blk_max, axis=-1, keepdims=True) + far)
                    alpha = jnp.exp2(m_prev - m_new)
                    shift = m_new - far
                    l_part = alpha * l_ref[pc, rows, :]
                    ps = []
                    for sj in blocks:
                        pj = jnp.exp2(sj - shift)
                        l_part = l_part + pj
                        ps.append(pj.astype(BF16))
                    l_ref[pc, rows, :] = l_part
                    m_ref[pc, rows, :] = m_new
                    pv = jnp.dot(jnp.concatenate(ps, axis=1), v, preferred_element_type=F32)
                    acc_ref[pc, rows, :] = (jnp.concatenate([alpha] * (W // 128), axis=1)
                                            * acc_ref[pc, rows, :] + pv)

    @pl.when(qi - ki >= 2)
    def _():
        step("far")

    @pl.when(qi - ki == 1)
    def _():
        step("near")

    @pl.when(qi == ki)
    def _():
        step("diag")
        lam_init = li_ref[0]
        dl = dl_ref[...]
        lam = (jnp.exp(jnp.sum(dl[0:1] * dl[1:2], axis=-1, keepdims=True))
               - jnp.exp(jnp.sum(dl[2:3] * dl[3:4], axis=-1, keepdims=True)) + lam_init)
        for hh in range(hb):
            l0 = jnp.sum(l_ref[2 * hh], axis=-1, keepdims=True)
            l1 = jnp.sum(l_ref[2 * hh + 1], axis=-1, keepdims=True)
            o = acc_ref[2 * hh] / l0 - lam * (acc_ref[2 * hh + 1] / l1)
            ms = jnp.mean(o * o, axis=-1, keepdims=True)
            y = (o * lax.rsqrt(ms + EPS) * sub_ref[...]) * (1.0 - lam_init)
            o_ref[:, hh * W:(hh + 1) * W] = y.astype(o_ref.dtype)


def _diff_attention(z, rel_bias, bias_band, diff_lambda, subln, lam_init, *, off_q, off_k, off_v,
                    n_heads, T):
    S = z.shape[0]
    W = 2 * HEAD_DIM
    R = bias_band.shape[1]
    nq = S // T
    qs, ks = [], []
    for qi in range(nq):
        for ki in range(qi + 1):
            qs.append(qi)
            ks.append(ki)
    qt = jnp.asarray(qs, jnp.int32)
    kt = jnp.asarray(ks, jnp.int32)
    hb = ATTN_HEADS_PER_STEP if n_heads % ATTN_HEADS_PER_STEP == 0 else 1
    WB = hb * W
    assert off_q % WB == 0 and off_k % WB == 0 and off_v % WB == 0
    bq, bk, bv = off_q // WB, off_k // WB, off_v // WB
    grid_spec = pltpu.PrefetchScalarGridSpec(
        num_scalar_prefetch=2,
        grid=(n_heads // hb, len(qs)),
        in_specs=[pl.BlockSpec(memory_space=pltpu.SMEM),
                  pl.BlockSpec(memory_space=pltpu.SMEM),
                  pl.BlockSpec((4, HEAD_DIM), lambda h, p, qt, kt: (0, 0)),
                  pl.BlockSpec((1, W), lambda h, p, qt, kt: (0, 0)),
                  pl.BlockSpec((T, WB), lambda h, p, qt, kt: (qt[p], bq + h)),
                  pl.BlockSpec((T, WB), lambda h, p, qt, kt: (kt[p], bk + h)),
                  pl.BlockSpec((T, WB), lambda h, p, qt, kt: (kt[p], bv + h)),
                  pl.BlockSpec((hb, R, 2 * R), lambda h, p, qt, kt: (h, 0, 0))],
        out_specs=pl.BlockSpec((T, WB), lambda h, p, qt, kt: (qt[p], h)),
        scratch_shapes=[pltpu.VMEM((2 * hb, T, 128), F32), pltpu.VMEM((2 * hb, T, 128), F32),
                        pltpu.VMEM((2 * hb, T, W), F32)],
    )
    est = (2 * (4 * T * WB * 2 + hb * 2 * R * R * 4) + 2 * hb * T * W * 4 + 4 * hb * T * 128 * 4
           + 16 * R * T * 4)
    return pl.pallas_call(
        functools.partial(_attn_kernel, row_chunks=T // R),
        out_shape=jax.ShapeDtypeStruct((S, n_heads * W), BF16),
        grid_spec=grid_spec,
        compiler_params=_cparams(2, est),
        name="diff_attention",
    )(qt, kt, rel_bias, jnp.full((1,), lam_init, F32), diff_lambda, subln.reshape(1, W),
      z, z, z, bias_band)


def _ret_kernel(cd_ref, q_ref, k_ref, v_ref, g_ref, cos_ref, sin_ref, dm_ref, xi_ref, ze_ref,
                o_ref, R_ref):
    t = pl.program_id(0)
    n_heads = R_ref.shape[0]

    @pl.when(t == 0)
    def _():
        R_ref[...] = jnp.zeros(R_ref.shape, F32)

    half = HEAD_DIM // 2
    Rs = [R_ref[h] for h in range(n_heads)]
    for c in range(q_ref.shape[0] // CHUNK):
        sl = slice(c * CHUNK, (c + 1) * CHUNK)
        cs = cos_ref[sl, :]
        sn = sin_ref[sl, :]
        for h in range(n_heads):
            hc = slice(h * HEAD_DIM, (h + 1) * HEAD_DIM)
            q = q_ref[sl, hc].astype(F32)
            k = k_ref[sl, hc].astype(F32)
            v = v_ref[sl, hc]
            qr = q * cs + pltpu.roll(q, half, 1) * sn
            kr = (k * cs + pltpu.roll(k, half, 1) * sn) * (HEAD_DIM ** -0.5)
            qb = qr.astype(BF16)
            inner = lax.dot_general(qb, kr.astype(BF16), (((1,), (1,)), ((), ())),
                                    preferred_element_type=F32) * dm_ref[h]
            o = (jnp.dot(inner.astype(BF16), v, preferred_element_type=F32)
                 + jnp.dot(qb, Rs[h].astype(BF16), preferred_element_type=F32) * xi_ref[h])
            Rs[h] = Rs[h] * cd_ref[h] + lax.dot_general(
                (kr * ze_ref[h]).astype(BF16), v, (((0,), (0,)), ((), ())), preferred_element_type=F32)
            o = o * lax.rsqrt(jnp.mean(o * o, axis=-1, keepdims=True) + EPS)
            g = g_ref[sl, hc].astype(F32)
            o_ref[sl, hc] = (jax.nn.silu(g) * o).astype(o_ref.dtype)
    for h in range(n_heads):
        R_ref[h] = Rs[h]


def _retention_tables(S, n_heads):
    half = HEAD_DIM // 2
    pos = jnp.arange(S, dtype=F32)
    theta = 1.0 / (10000.0 ** jnp.linspace(0.0, 1.0, half, dtype=F32))
    ang = pos[:, None] * theta[None, :]
    cos = jnp.cos(ang)
    sin = jnp.sin(ang)
    cosf = jnp.concatenate([cos, cos], axis=-1)
    sinf = jnp.concatenate([-sin, sin], axis=-1)
    log_gamma = jnp.log1p(-jnp.exp2(-5.0 - jnp.arange(n_heads, dtype=F32)))
    idx = jnp.arange(CHUNK)
    diff = idx[:, None] - idx[None, :]
    dmat = jnp.where(diff >= 0,
                     jnp.exp(log_gamma[:, None, None] * jnp.maximum(diff, 0).astype(F32)), 0.0)
    xi = jnp.exp(log_gamma[:, None] * (idx[None, :] + 1).astype(F32))[..., None]
    zeta = jnp.exp(log_gamma[:, None] * (CHUNK - 1 - idx)[None, :].astype(F32))[..., None]
    chunk_decay = jnp.exp(log_gamma * CHUNK)
    return cosf, sinf, dmat, xi, zeta, chunk_decay


def _retention(z, tables, *, off_q, off_k, off_v, off_g, n_heads, TR):
    S = z.shape[0]
    cosf, sinf, dmat, xi, zeta, chunk_decay = tables
    WR = n_heads * HEAD_DIM
    assert all(o % WR == 0 for o in (off_q, off_k, off_v, off_g))
    blk = lambda off: pl.BlockSpec((TR, WR), lambda t: (t, off // WR))
    tab = pl.BlockSpec((TR, HEAD_DIM), lambda t: (t, 0))
    whole = lambda last: pl.BlockSpec((n_heads, CHUNK, last), lambda t: (0, 0, 0))
    est = (2 * (5 * TR * WR * 2 + 2 * TR * HEAD_DIM * 4 + n_heads * 3 * CHUNK * CHUNK * 4)
           + n_heads * 16 * CHUNK * CHUNK * 4)
    return pl.pallas_call(
        _ret_kernel,
        out_shape=jax.ShapeDtypeStruct((S, WR), BF16),
        grid=(S // TR,),
        in_specs=[pl.BlockSpec(memory_space=pltpu.SMEM),
                  blk(off_q), blk(off_k), blk(off_v), blk(off_g), tab, tab,
                  whole(CHUNK), whole(1), whole(1)],
        out_specs=pl.BlockSpec((TR, WR), lambda t: (t, 0)),
        scratch_shapes=[pltpu.VMEM((n_heads, HEAD_DIM, HEAD_DIM), F32)],
        compiler_params=_cparams(1, est),
        name="retention",
    )(chunk_decay, z, z, z, z, cosf, sinf, dmat, xi, zeta)


def _s5c_kernel(uf_ref, e_ref, f_ref, a_ref, apr_ref, api_ref, d_ref, o_ref, carry_ref):
    t = pl.program_id(1)
    Nc = uf_ref.shape[0]
    C = apr_ref.shape[2]
    sg = a_ref.shape[1]
    gw = a_ref.shape[2]

    @pl.when(t == 0)
    def _():
        carry_ref[...] = jnp.zeros(carry_ref.shape, F32)

    uf = uf_ref[...]
    x = jnp.dot(uf, e_ref[0], preferred_element_type=F32)
    ng = Nc // 8
    hr = x[:, :C].reshape(ng, 8, C)
    hi = x[:, C:].reshape(ng, 8, C)
    apr = apr_ref[0]
    api = api_ref[0]
    sub = lax.broadcasted_iota(jnp.int32, (ng, 8, C), 1)
    for d in (1, 2, 4):
        a_r = apr[d - 1:d, :].reshape(1, 1, C)
        a_i = api[d - 1:d, :].reshape(1, 1, C)
        sr = pltpu.roll(hr, d, 1)
        si = pltpu.roll(hi, d, 1)
        m = sub >= d
        nr = hr + jnp.where(m, a_r * sr - a_i * si, 0.0)
        ni = hi + jnp.where(m, a_r * si + a_i * sr, 0.0)
        hr, hi = nr, ni
    cin = carry_ref[...]
    c_r = cin[:, :C]
    c_i = cin[:, C:]
    out_r, out_i = [], []
    for g in range(ng):
        b_r = jnp.broadcast_to(c_r, (8, C))
        b_i = jnp.broadcast_to(c_i, (8, C))
        g_r = hr[g] + (apr * b_r - api * b_i)
        g_i = hi[g] + (apr * b_i + api * b_r)
        out_r.append(g_r)
        out_i.append(g_i)
        c_r = g_r[7:8, :]
        c_i = g_i[7:8, :]
    carry_ref[...] = jnp.concatenate([c_r, c_i], axis=1)
    h_end = jnp.concatenate([jnp.concatenate(out_r, axis=0), jnp.concatenate(out_i, axis=0)], axis=1)
    row = lax.broadcasted_iota(jnp.int32, (Nc, 2 * C), 0)
    h_in = jnp.where(row == 0, cin, pltpu.roll(h_end, 1, 0)).astype(BF16)
    y = jnp.dot(h_in, f_ref[0], preferred_element_type=F32)
    y = y + jnp.concatenate(
        [jnp.dot(uf[:, g * gw:(g + 1) * gw], a_ref[0, g], preferred_element_type=F32)
         for g in range(sg)], axis=1)
    y = y + d_ref[0] * uf.astype(F32)
    o_ref[...] = jax.nn.gelu(y).astype(o_ref.dtype)


def _s5c_params(lam_re, lam_im, log_dt, b_re, b_im, c_re, c_im, L):
    G, N = lam_re.shape
    sg = S5_SLICE_GROUPS
    nsl = G // sg
    J = S5_GROUP
    lr, li = lam_re.astype(F32), lam_im.astype(F32)
    dt = jnp.exp(log_dt.astype(F32))[:, None]
    mag = jnp.exp(lr * dt)
    ar = mag * jnp.cos(li * dt)
    ai = mag * jnp.sin(li * dt)
    den = lr * lr + li * li
    fr = ((ar - 1.0) * lr + ai * li) / den
    fi = (ai * lr - (ar - 1.0) * li) / den
    br_, bi_ = b_re.astype(F32), b_im.astype(F32)
    bbr = fr[..., None] * br_ - fi[..., None] * bi_
    bbi = fr[..., None] * bi_ + fi[..., None] * br_
    cr, ci = c_re.astype(F32), c_im.astype(F32)
    pr, pi = [jnp.ones_like(ar)], [jnp.zeros_like(ar)]
    for _ in range(L):
        pr.append(pr[-1] * ar - pi[-1] * ai)
        pi.append(pr[-2] * ai + pi[-1] * ar)
    P_r, P_i = jnp.stack(pr), jnp.stack(pi)
    ca_r = cr[None] * P_r[:, :, None, :] - ci[None] * P_i[:, :, None, :]
    ca_i = cr[None] * P_i[:, :, None, :] + ci[None] * P_r[:, :, None, :]
    K = (jnp.einsum('tgon,gni->tgoi', ca_r[:L], bbr) - jnp.einsum('tgon,gni->tgoi', ca_i[:L], bbi))
    lag = jnp.arange(L)[None, :] - jnp.arange(L)[:, None]
    A = jnp.where((lag >= 0)[:, :, None, None, None], K[jnp.clip(lag, 0, L - 1)], 0.0)
    A = A.transpose(2, 0, 4, 1, 3).reshape(nsl, sg, L * J, L * J)
    rev = P_r[L - 1::-1][:L], P_i[L - 1::-1][:L]
    e_r = rev[0][:, :, :, None] * bbr[None] - rev[1][:, :, :, None] * bbi[None]
    e_i = rev[0][:, :, :, None] * bbi[None] + rev[1][:, :, :, None] * bbr[None]
    eye = jnp.eye(sg, dtype=F32)

    def enc(e):
        e = e.transpose(1, 0, 3, 2).reshape(nsl, sg, L * J, N)
        return jnp.einsum('sgrn,gh->sgrhn', e, eye).reshape(nsl, sg * L * J, sg * N)

    def dec(f):
        f = f.transpose(1, 3, 0, 2).reshape(nsl, sg, N, L * J)
        return jnp.einsum('sgnc,gh->sgnhc', f, eye).reshape(nsl, sg * N, sg * L * J)

    E = jnp.concatenate([enc(e_r), enc(e_i)], axis=-1).astype(BF16)
    Fm = jnp.concatenate([dec(ca_r[1:]), -dec(ca_i[1:])], axis=1).astype(BF16)
    qr, qi = [P_r[L]], [P_i[L]]
    for _ in range(7):
        qr.append(qr[-1] * P_r[L] - qi[-1] * P_i[L])
        qi.append(qr[-2] * P_i[L] + qi[-1] * P_r[L])
    apr = jnp.stack(qr).reshape(8, nsl, sg * N).transpose(1, 0, 2)
    api = jnp.stack(qi).reshape(8, nsl, sg * N).transpose(1, 0, 2)
    return E, Fm, A.astype(BF16), apr, api


def _s5c_scan(uf, s5p, d_skip, *, L, Nc):
    nchunks, width = uf.shape
    E, Fm, A, apr, api = s5p
    nsl, rows, xw = E.shape
    C = xw // 2
    sg, gw = A.shape[1], A.shape[2]
    dflat = jnp.tile(d_skip.astype(F32).reshape(nsl, sg, 1, S5_GROUP), (1, 1, L, 1)).reshape(nsl, 1, rows)
    est = (2 * (2 * Nc * rows * 2 + 2 * rows * xw * 2 + sg * gw * gw * 2 + 2 * 8 * C * 4 + rows * 4)
           + 10 * Nc * xw * 4 + 4 * Nc * rows * 4)
    return pl.pallas_call(
        _s5c_kernel,
        out_shape=jax.ShapeDtypeStruct((nchunks, width), BF16),
        grid=(nsl, nchunks // Nc),
        in_specs=[pl.BlockSpec((Nc, rows), lambda j, t: (t, j)),
                  pl.BlockSpec((1, rows, xw), lambda j, t: (j, 0, 0)),
                  pl.BlockSpec((1, xw, rows), lambda j, t: (j, 0, 0)),
                  pl.BlockSpec((1, sg, gw, gw), lambda j, t: (j, 0, 0, 0)),
                  pl.BlockSpec((1, 8, C), lambda j, t: (j, 0, 0)),
                  pl.BlockSpec((1, 8, C), lambda j, t: (j, 0, 0)),
                  pl.BlockSpec((1, 1, rows), lambda j, t: (j, 0, 0))],
        out_specs=pl.BlockSpec((Nc, rows), lambda j, t: (t, j)),
        scratch_shapes=[pltpu.VMEM((1, xw), F32)],
        compiler_params=_cparams(2, est),
        name="s5_chunk_scan",
    )(uf, E, Fm, A, apr, api, dflat)


def _glu_kernel(x_ref, w_ref, o_ref):
    zg = jnp.dot(x_ref[...], w_ref[...], preferred_element_type=F32)
    W = zg.shape[1] // 2
    o_ref[...] = (zg[:, :W] * jax.nn.sigmoid(zg[:, W:])).astype(o_ref.dtype)


def _glu(x, w, layer, *, bm):
    M, K = x.shape
    N2 = w.shape[2]
    est = 2 * (bm * K * 2 + K * N2 * 2 + bm * N2) + 3 * bm * N2 * 4
    return pl.pallas_call(
        _glu_kernel,
        out_shape=jax.ShapeDtypeStruct((M, N2 // 2), BF16),
        grid=(M // bm,),
        in_specs=[pl.BlockSpec((bm, K), lambda i: (i, 0)),
                  pl.BlockSpec((None, K, N2), lambda i: (layer, 0, 0))],
        out_specs=pl.BlockSpec((bm, N2 // 2), lambda i: (i, 0)),
        compiler_params=_cparams(1, est),
        name="s5_glu",
    )(x, w)


def _s5_chunked(z, s5p, d_skip, wglu, layer, *, off_u, width, L, Nc, bm):
    S = z.shape[0]
    G = width // S5_GROUP
    u = z[:, off_u:off_u + width].reshape(S // L, L, G, S5_GROUP)
    uf = u.transpose(0, 2, 1, 3).reshape(S // L, G * L * S5_GROUP)
    yf = _s5c_scan(uf, s5p, d_skip, L=L, Nc=Nc)
    zc = yf.reshape(S // L, G, L, S5_GROUP).transpose(0, 2, 1, 3).reshape(S, width)
    return _glu(zc, wglu, layer, bm=bm)


def _merge_kernel(gl_ref, od_ref, or_ref, os_ref, wg0_ref, wg1_ref, wg2_ref, wd_ref, wr_ref, ws_ref,
                  o_ref):
    gl = gl_ref[...]

    def branch(wg_ref, x_ref, w_ref):
        gate = jax.nn.sigmoid(jnp.dot(gl, wg_ref[...], preferred_element_type=F32))
        return gate * jnp.dot(x_ref[...], w_ref[...], preferred_element_type=F32)

    o_ref[...] = (branch(wg0_ref, od_ref, wd_ref) + branch(wg1_ref, or_ref, wr_ref)
                  + branch(wg2_ref, os_ref, ws_ref)).astype(o_ref.dtype)


def _merge(z, o_diff, o_ret, o_s5, wg, wd, wr, ws, layer, *, off_gate, bm, bn):
    S = z.shape[0]
    R = wg.shape[1]
    D = wd.shape[2]
    nb = D // bn
    kd, kr, ks = wd.shape[1], wr.shape[1], ws.shape[1]
    row = lambda k: pl.BlockSpec((bm, k), lambda i, j: (i, 0))
    col = lambda k: _wspec(layer, k, bn)
    est = 2 * (bm * (R + kd + kr + ks) * 2 + (3 * R + kd + kr + ks) * bn * 2 + bm * bn * 2) + 8 * bm * bn * 4
    return pl.pallas_call(
        _merge_kernel,
        out_shape=jax.ShapeDtypeStruct((S, D), BF16),
        grid=(S // bm, nb),
        in_specs=[pl.BlockSpec((bm, R), lambda i, j: (i, off_gate // R)),
                  row(kd), row(kr), row(ks),
                  _wspec(layer, R, bn),
                  _wspec(layer, R, bn, lambda j: nb + j),
                  _wspec(layer, R, bn, lambda j: 2 * nb + j),
                  col(kd), col(kr), col(ks)],
        out_specs=pl.BlockSpec((bm, bn), lambda i, j: (i, j)),
        compiler_params=_cparams(2, est),
        name="gated_merge",
    )(z, o_diff, o_ret, o_s5, wg, wg, wg, wd, wr, ws)


def _mm_res_kernel(x_ref, w_ref, h_ref, o_ref):
    o_ref[...] = h_ref[...] + jnp.dot(x_ref[...], w_ref[...], preferred_element_type=F32)


def _mm_res_norm_kernel(x_ref, w_ref, h_ref, g_ref, o_ref, hg_ref, ss_ref):
    hn = h_ref[...] + jnp.dot(x_ref[...], w_ref[...], preferred_element_type=F32)
    o_ref[...] = hn
    hg_ref[...] = (hn * g_ref[...]).astype(hg_ref.dtype)
    part = _lane_partial_sumsq(hn)

    @pl.when(pl.program_id(1) == 0)
    def _():
        ss_ref[...] = part

    @pl.when(pl.program_id(1) > 0)
    def _():
        ss_ref[...] += part


def _mm_res(x, w, layer, h, *, bm, bn, in_place, next_gain=None):
    M, K = x.shape
    N = w.shape[2]
    est = 2 * (bm * K * 2 + K * bn * 2 + 2 * bm * bn * 4 + bm * bn * 2 + bm * 512) + 2 * bm * bn * 4
    in_specs = [pl.BlockSpec((bm, K), lambda i, j: (i, 0)),
                _wspec(layer, K, bn),
                pl.BlockSpec((bm, bn), lambda i, j: (i, j))]
    h_spec = pl.BlockSpec((bm, bn), lambda i, j: (i, j))
    common = dict(grid=(M // bm, N // bn), input_output_aliases={2: 0} if in_place else {},
                  compiler_params=_cparams(2, est))
    if next_gain is None:
        return pl.pallas_call(
            _mm_res_kernel, out_shape=jax.ShapeDtypeStruct((M, N), F32), in_specs=in_specs,
            out_specs=h_spec, name="mm_residual", **common)(x, w, h)
    return pl.pallas_call(
        _mm_res_norm_kernel,
        out_shape=(jax.ShapeDtypeStruct((M, N), F32), jax.ShapeDtypeStruct((M, N), BF16),
                   jax.ShapeDtypeStruct((M, 128), F32)),
        in_specs=in_specs + [pl.BlockSpec((1, bn), lambda i, j: (0, j))],
        out_specs=(h_spec, pl.BlockSpec((bm, bn), lambda i, j: (i, j)),
                   pl.BlockSpec((bm, 128), lambda i, j: (i, 0))),
        name="mm_residual_norm", **common)(x, w, h, next_gain.reshape(1, N))


def _ffn_up_kernel(hg_ref, ss_ref, wg_ref, wu_ref, o_ref):
    hg = hg_ref[...]
    r = _row_rsqrt(ss_ref, hg_ref.shape[1])
    a = jnp.dot(hg, wg_ref[...], preferred_element_type=F32) * r
    b = jnp.dot(hg, wu_ref[...], preferred_element_type=F32) * r
    o_ref[...] = (jax.nn.silu(a) * b).astype(o_ref.dtype)


def _ffn_up(hg, ss, wg, wu, layer, *, bm, bn):
    M, K = hg.shape
    N = wg.shape[2]
    est = 2 * (bm * K * 2 + bm * 512 + 2 * K * bn * 2 + bm * bn * 2) + 5 * bm * bn * 4
    return pl.pallas_call(
        _ffn_up_kernel,
        out_shape=jax.ShapeDtypeStruct((M, N), BF16),
        grid=(M // bm, pl.cdiv(N, bn)),
        in_specs=[pl.BlockSpec((bm, K), lambda i, j: (i, 0)),
                  pl.BlockSpec((bm, 128), lambda i, j: (i, 0)),
                  _wspec(layer, K, bn),
                  _wspec(layer, K, bn)],
        out_specs=pl.BlockSpec((bm, bn), lambda i, j: (i, j)),
        compiler_params=_cparams(2, est),
        name="ffn_up",
    )(hg, ss, wg, wu)


def _ple_update(h_ref, g_ref, p_ref, wple_ref, wgd_ref, wgu_ref, hp_ref):
    _rms_rows(h_ref, g_ref, hp_ref, min(64, h_ref.shape[0]))
    t = jnp.dot(hp_ref[...], wgd_ref[...], preferred_element_type=F32).astype(BF16)
    gate = jax.nn.sigmoid(jnp.dot(t, wgu_ref[...], preferred_element_type=F32))
    e = jnp.dot(p_ref[...].astype(BF16), wple_ref[...], preferred_element_type=F32)
    return h_ref[...] + e * gate


def _ple_kernel(h_ref, g_ref, p_ref, wple_ref, wgd_ref, wgu_ref, gn_ref, o_ref, hg_ref, ss_ref, hp_ref):
    hn = _ple_update(h_ref, g_ref, p_ref, wple_ref, wgd_ref, wgu_ref, hp_ref)
    o_ref[...] = hn
    hg_ref[...] = (hn * gn_ref[...]).astype(hg_ref.dtype)
    ss_ref[...] = _lane_partial_sumsq(hn)


def _ple_final_kernel(h_ref, g_ref, p_ref, wple_ref, wgd_ref, wgu_ref, gn_ref, o_ref, hp_ref):
    hn = _ple_update(h_ref, g_ref, p_ref, wple_ref, wgd_ref, wgu_ref, hp_ref)
    ms = jnp.mean(hn * hn, axis=-1, keepdims=True)
    o_ref[...] = hn * lax.rsqrt(ms + EPS) * gn_ref[...]


def _ple(h, gain, p, wple, wgd, wgu, layer, next_gain, *, bm, final):
    M, D = h.shape
    P = p.shape[2]
    R = wgd.shape[2]
    est = 2 * (2 * bm * D * 4 + bm * D * 2 + bm * P * 4 + (P + 2 * R) * D * 2) + bm * D * 2 + 4 * bm * D * 4
    row = pl.BlockSpec((bm, D), lambda i: (i, 0))
    in_specs = [row,
                pl.BlockSpec((1, D), lambda i: (0, 0)),
                pl.BlockSpec((None, bm, P), lambda i: (layer, i, 0)),
                pl.BlockSpec((None, P, D), lambda i: (layer, 0, 0)),
                pl.BlockSpec((None, D, R), lambda i: (layer, 0, 0)),
                pl.BlockSpec((None, R, D), lambda i: (layer, 0, 0)),
                pl.BlockSpec((1, D), lambda i: (0, 0))]
    common = dict(grid=(M // bm,), in_specs=in_specs, scratch_shapes=[pltpu.VMEM((bm, D), BF16)],
                  input_output_aliases={0: 0}, compiler_params=_cparams(1, est))
    args = (h, gain.reshape(1, D), p, wple, wgd, wgu, next_gain.reshape(1, D))
    if final:
        return pl.pallas_call(_ple_final_kernel, out_shape=jax.ShapeDtypeStruct((M, D), F32),
                              out_specs=row, name="ple_final", **common)(*args)
    return pl.pallas_call(
        _ple_kernel,
        out_shape=(jax.ShapeDtypeStruct((M, D), F32), jax.ShapeDtypeStruct((M, D), BF16),
                   jax.ShapeDtypeStruct((M, 128), F32)),
        out_specs=(row, row, pl.BlockSpec((bm, 128), lambda i: (i, 0))),
        name="ple", **common)(*args)


def kernel(x, p, rel_bias, norm_mix, w_in, diff_lambda, diff_subln, s5_lambda_re, s5_lambda_im, s5_log_dt, s5_b_re, s5_b_im, s5_c_re, s5_c_im, s5_d, s5_w_glu, w_gate_up, w_br_diff, w_br_ret, w_br_s5, w_o, norm_ffn, w_ffn_gate, w_ffn_up, w_ffn_down, norm_ple, w_ple, w_ple_gate_down, w_ple_gate_up, norm_final):
    B, S, D = x.shape
    assert B == 1
    depth = w_in.shape[0]
    diff_w, ret_w, s5_w = w_br_diff.shape[1], w_br_ret.shape[1], w_br_s5.shape[1]
    gate_rank = w_gate_up.shape[1]
    diff_heads = diff_w // (2 * HEAD_DIM)
    ret_heads = ret_w // HEAD_DIM
    off_dq = 0
    off_dk = off_dq + diff_w
    off_dv = off_dk + diff_w
    off_rq = off_dv + diff_w
    off_rk = off_rq + ret_w
    off_rv = off_rk + ret_w
    off_rg = off_rv + ret_w
    off_su = off_rg + ret_w
    off_gate = off_su + s5_w
    in_w = off_gate + gate_rank
    assert w_in.shape[2] == in_w
    hidden = w_ffn_gate.shape[2]

    T_attn = _tile(S, 1024)
    T_ret = _tile(S, 256)
    bm_in = _tile(S, 1024)
    bn_in = _tile(in_w, 768) if in_w % 768 == 0 else _tile(in_w, 512)
    bm_mg, bn_mg = _tile(S, 1024), _tile(D, 512)
    bm_o, bn_o = _tile(S, 1024), _tile(D, 512)
    bm_up, bn_up = _tile(S, 1024), min(hidden, 512)
    bm_dn, bn_dn = _tile(S, 512), _tile(D, 512)
    bm_ple = _tile(S, 256)

    h = x.reshape(S, D)
    hg, ss = _row_prep(h, norm_mix[0], bm=bm_ple)
    bias_band = _bias_band(rel_bias, diff_heads, min(ATTN_ROW_CHUNK, T_attn))
    ret_tables = _retention_tables(S, ret_heads)
    z_scale = jnp.where(jnp.arange(in_w) < off_dk, HEAD_DIM ** -0.5 * LOG2E, 1.0).astype(F32)

    (w_in_b, w_glu_b, w_gate_b, w_brd_b, w_brr_b, w_brs_b, w_o_b, w_fg_b, w_fu_b, w_fd_b, w_ple_b,
     w_pgd_b, w_pgu_b) = (w.astype(BF16) for w in (
         w_in, s5_w_glu, w_gate_up, w_br_diff, w_br_ret, w_br_s5, w_o, w_ffn_gate, w_ffn_up,
         w_ffn_down, w_ple, w_ple_gate_down, w_ple_gate_up))
    p3 = p.reshape(depth, S, p.shape[-1])

    for i in range(depth):
        lam_init = 0.8 - 0.6 * math.exp(-0.3 * i)
        z = _norm_mm(hg, ss, w_in_b, i, z_scale, bm=bm_in, bn=bn_in)
        o_diff = _diff_attention(z, rel_bias, bias_band, diff_lambda[i], diff_subln[i], lam_init,
                                 off_q=off_dq, off_k=off_dk, off_v=off_dv, n_heads=diff_heads, T=T_attn)
        o_ret = _retention(z, ret_tables, off_q=off_rq, off_k=off_rk, off_v=off_rv, off_g=off_rg,
                           n_heads=ret_heads, TR=T_ret)
        s5p = _s5c_params(s5_lambda_re[i], s5_lambda_im[i], s5_log_dt[i], s5_b_re[i], s5_b_im[i],
                          s5_c_re[i], s5_c_im[i], S5_CHUNK)
        o_s5 = _s5_chunked(z, s5p, s5_d[i], w_glu_b, i, off_u=off_su, width=s5_w, L=S5_CHUNK,
                           Nc=_tile(S // S5_CHUNK, 512), bm=_tile(S, 1024))
        mixed = _merge(z, o_diff, o_ret, o_s5, w_gate_b, w_brd_b, w_brr_b, w_brs_b, i,
                       off_gate=off_gate, bm=bm_mg, bn=bn_mg)
        h, hg, ss = _mm_res(mixed, w_o_b, i, h, bm=bm_o, bn=bn_o, in_place=i > 0, next_gain=norm_ffn[i])
        act = _ffn_up(hg, ss, w_fg_b, w_fu_b, i, bm=bm_up, bn=bn_up)
        h = _mm_res(act, w_fd_b, i, h, bm=bm_dn, bn=bn_dn, in_place=True)
        if i + 1 < depth:
            h, hg, ss = _ple(h, norm_ple[i], p3, w_ple_b, w_pgd_b, w_pgu_b, i, norm_mix[i + 1],
                             bm=bm_ple, final=False)
        else:
            out = _ple(h, norm_ple[i], p3, w_ple_b, w_pgd_b, w_pgu_b, i, norm_final,
                       bm=bm_ple, final=True)
    return out.reshape(B, S, D)
```

```python
import functools
import math

import jax
import jax.numpy as jnp
from jax import lax
from jax.experimental import pallas as pl
from jax.experimental.pallas import tpu as pltpu

F32 = jnp.float32
BF16 = jnp.bfloat16
EPS = 1e-6

HEAD_DIM = 128
CHUNK = 128
REL_BUCKETS = 32
REL_MAX_DIST = 128
S5_GROUP = 16
S5_STATE = 64
S5_SLICE_GROUPS = 8
S5_CHUNK = 16
MASK_VALUE = -1e30
LOG2E = math.log2(math.e)
ATTN_ROW_CHUNK = 128
ATTN_HEADS_PER_STEP = 2

V7X_VMEM_BYTES = 64 * 1024 * 1024
V7X_VMEM_REQUEST_CAP = 60 * 1024 * 1024
SPILL_AND_TEMP_ALLOWANCE = 8 * 1024 * 1024


def _cparams(n_axes, est_bytes):
    limit = int(min(est_bytes + SPILL_AND_TEMP_ALLOWANCE, V7X_VMEM_REQUEST_CAP))
    return pltpu.CompilerParams(dimension_semantics=("arbitrary",) * n_axes, vmem_limit_bytes=limit)


def _tile(n, want):
    t = min(n, want)
    while n % t:
        t //= 2
    return t


def _rms_rows(x_ref, g_ref, dst_ref, rows):
    n = x_ref.shape[0] // rows

    def body(i, c):
        r = pl.multiple_of(i * rows, rows)
        x = x_ref[pl.ds(r, rows), :]
        ms = jnp.mean(x * x, axis=-1, keepdims=True)
        dst_ref[pl.ds(r, rows), :] = (x * lax.rsqrt(ms + EPS) * g_ref[...]).astype(dst_ref.dtype)
        return c

    lax.fori_loop(0, n, body, 0)


def _lane_partial_sumsq(x):
    sq = x * x
    part = sq[:, 0:128]
    for j in range(1, x.shape[1] // 128):
        part = part + sq[:, j * 128:(j + 1) * 128]
    return part


def _row_rsqrt(ss_ref, n_cols):
    return lax.rsqrt(jnp.sum(ss_ref[...], axis=-1, keepdims=True) * (1.0 / n_cols) + EPS)


def _row_prep_kernel(x_ref, g_ref, hg_ref, ss_ref):
    x = x_ref[...]
    hg_ref[...] = (x * g_ref[...]).astype(hg_ref.dtype)
    ss_ref[...] = _lane_partial_sumsq(x)


def _row_prep(x, gain, *, bm):
    M, D = x.shape
    return pl.pallas_call(
        _row_prep_kernel,
        out_shape=(jax.ShapeDtypeStruct((M, D), BF16), jax.ShapeDtypeStruct((M, 128), F32)),
        grid=(M // bm,),
        in_specs=[pl.BlockSpec((bm, D), lambda i: (i, 0)), pl.BlockSpec((1, D), lambda i: (0, 0))],
        out_specs=(pl.BlockSpec((bm, D), lambda i: (i, 0)), pl.BlockSpec((bm, 128), lambda i: (i, 0))),
        compiler_params=_cparams(1, 2 * (bm * D * 6 + bm * 512) + 2 * bm * D * 4),
        name="row_prep",
    )(x, gain.reshape(1, D))


def _norm_mm_kernel(hg_ref, ss_ref, w_ref, cs_ref, o_ref):
    acc = jnp.dot(hg_ref[...], w_ref[...], preferred_element_type=F32)
    r = _row_rsqrt(ss_ref, hg_ref.shape[1])
    o_ref[...] = ((acc * r) * cs_ref[...]).astype(o_ref.dtype)


def _wspec(layer, k, bn, col_block=lambda j: j):
    return pl.BlockSpec((None, k, bn), lambda i, j: (layer, 0, col_block(j)))


def _norm_mm(hg, ss, w, layer, col_scale, *, bm, bn, out_dtype=BF16):
    M, K = hg.shape
    N = w.shape[2]
    est = 2 * (bm * K * 2 + bm * 512 + K * bn * 2 + bm * bn * 2) + 4 * bm * bn * 4
    return pl.pallas_call(
        _norm_mm_kernel,
        out_shape=jax.ShapeDtypeStruct((M, N), out_dtype),
        grid=(M // bm, N // bn),
        in_specs=[pl.BlockSpec((bm, K), lambda i, j: (i, 0)),
                  pl.BlockSpec((bm, 128), lambda i, j: (i, 0)),
                  _wspec(layer, K, bn),
                  pl.BlockSpec((1, bn), lambda i, j: (0, j))],
        out_specs=pl.BlockSpec((bm, bn), lambda i, j: (i, j)),
        compiler_params=_cparams(2, est),
        name="norm_mm",
    )(hg, ss, w, col_scale.reshape(1, N))


def _bias_band_kernel(rb_ref, o_ref):
    h = pl.program_id(0)
    R = o_ref.shape[1]
    r = lax.broadcasted_iota(jnp.int32, (R, 2 * R), 0)
    c = lax.broadcasted_iota(jnp.int32, (R, 2 * R), 1)
    rel = R + r - c
    n = jnp.maximum(rel, 0)
    max_exact = REL_BUCKETS // 2
    nf = jnp.maximum(n, 1).astype(F32)
    large = max_exact + (jnp.log(nf / max_exact) / math.log(REL_MAX_DIST / max_exact)
                         * (REL_BUCKETS - max_exact)).astype(jnp.int32)
    large = jnp.minimum(large, REL_BUCKETS - 1)
    bucket = jnp.where(n < max_exact, n, large)
    val = jnp.zeros((R, 2 * R), F32)
    for b in range(REL_BUCKETS):
        val = jnp.where(bucket == b, rb_ref[b, h], val)
    far = rb_ref[REL_BUCKETS - 1, h]
    o_ref[0] = jnp.where(rel >= 0, (val - far) * LOG2E, MASK_VALUE)


def _bias_band(rel_bias, n_heads, R):
    assert R >= REL_MAX_DIST
    return pl.pallas_call(
        _bias_band_kernel,
        out_shape=jax.ShapeDtypeStruct((n_heads, R, 2 * R), F32),
        grid=(n_heads,),
        in_specs=[pl.BlockSpec(memory_space=pltpu.SMEM)],
        out_specs=pl.BlockSpec((1, R, 2 * R), lambda h: (h, 0, 0)),
        compiler_params=_cparams(1, 24 * R * R * 4),
        name="t5_bias_band",
    )(rel_bias)


def _attn_kernel(qt_ref, kt_ref, rb_ref, li_ref, dl_ref, sub_ref, q_ref, k_ref, v_ref, bt_ref,
                 o_ref, m_ref, l_ref, acc_ref, *, row_chunks):
    hg = pl.program_id(0)
    p = pl.program_id(1)
    qi = qt_ref[p]
    ki = kt_ref[p]
    T = q_ref.shape[0]
    W = 2 * HEAD_DIM
    hb = q_ref.shape[1] // W

    @pl.when(ki == 0)
    def _():
        m_ref[...] = jnp.full(m_ref.shape, MASK_VALUE, F32)
        l_ref[...] = jnp.zeros(l_ref.shape, F32)
        acc_ref[...] = jnp.zeros(acc_ref.shape, F32)

    R = T // row_chunks
    nb = R // 128

    def step(kind):
        for r in range(row_chunks):
            rows = slice(r * R, (r + 1) * R)
            ncol = (r + 1) * R if kind == "diag" else T
            ncb = ncol // 128
            if kind == "diag":
                band0 = ncb - 2 * nb
            elif kind == "near" and r == 0:
                band0 = ncb - nb
            else:
                band0 = None
            for hh in range(hb):
                v = v_ref[0:ncol, hh * W:(hh + 1) * W]
                far = rb_ref[REL_BUCKETS - 1, hg * hb + hh] * LOG2E
                for c in range(2):
                    pc = 2 * hh + c
                    col0 = hh * W + c * HEAD_DIM
                    qc = q_ref[rows, col0:col0 + HEAD_DIM]
                    kc = k_ref[0:ncol, col0:col0 + HEAD_DIM]
                    s = lax.dot_general(qc, kc, (((1,), (1,)), ((), ())), preferred_element_type=F32)
                    blocks = []
                    for j in range(ncb):
                        sj = s[:, j * 128:(j + 1) * 128]
                        if band0 is not None and j - band0 >= 0:
                            jb = j - band0
                            sj = sj + bt_ref[hh, :, jb * 128:(jb + 1) * 128]
                        blocks.append(sj)
                    blk_max = blocks[0]
                    for sj in blocks[1:]:
                        blk_max = jnp.maximum(blk_max, sj)
                    m_prev = m_ref[pc, rows, :]
                    m_new = jnp.maximum(m_prev, jnp.max(blk_max, axis=-1, keepdims=True) + far)
                    alpha = jnp.exp2(m_prev - m_new)
                    shift = m_new - far
                    l_part = alpha * l_ref[pc, rows, :]
                    ps = []
                    for sj in blocks:
                        pj = jnp.exp2(sj - shift)
                        l_part = l_part + pj
                        ps.append(pj.astype(BF16))
                    l_ref[pc, rows, :] = l_part
                    m_ref[pc, rows, :] = m_new
                    pv = jnp.dot(jnp.concatenate(ps, axis=1), v, preferred_element_type=F32)
                    acc_ref[pc, rows, :] = (jnp.concatenate([alpha] * (W // 128), axis=1)
                                            * acc_ref[pc, rows, :] + pv)

    @pl.when(qi - ki >= 2)
    def _():
        step("far")

    @pl.when(qi - ki == 1)
    def _():
        step("near")

    @pl.when(qi == ki)
    def _():
        step("diag")
        lam_init = li_ref[0]
        dl = dl_ref[...]
        lam = (jnp.exp(jnp.sum(dl[0:1] * dl[1:2], axis=-1, keepdims=True))
               - jnp.exp(jnp.sum(dl[2:3] * dl[3:4], axis=-1, keepdims=True)) + lam_init)
        for hh in range(hb):
            l0 = jnp.sum(l_ref[2 * hh], axis=-1, keepdims=True)
            l1 = jnp.sum(l_ref[2 * hh + 1], axis=-1, keepdims=True)
            o = acc_ref[2 * hh] / l0 - lam * (acc_ref[2 * hh + 1] / l1)
            ms = jnp.mean(o * o, axis=-1, keepdims=True)
            y = (o * lax.rsqrt(ms + EPS) * sub_ref[...]) * (1.0 - lam_init)
            o_ref[:, hh * W:(hh + 1) * W] = y.astype(o_ref.dtype)


def _diff_attention(z, rel_bias, bias_band, diff_lambda, subln, lam_init, *, off_q, off_k, off_v,
                    n_heads, T):
    S = z.shape[0]
    W = 2 * HEAD_DIM
    R = bias_band.shape[1]
    nq = S // T
    qs, ks = [], []
    for qi in range(nq):
        for ki in range(qi + 1):
            qs.append(qi)
            ks.append(ki)
    qt = jnp.asarray(qs, jnp.int32)
    kt = jnp.asarray(ks, jnp.int32)
    hb = ATTN_HEADS_PER_STEP if n_heads % ATTN_HEADS_PER_STEP == 0 else 1
    WB = hb * W
    assert off_q % WB == 0 and off_k % WB == 0 and off_v % WB == 0
    bq, bk, bv = off_q // WB, off_k // WB, off_v // WB
    grid_spec = pltpu.PrefetchScalarGridSpec(
        num_scalar_prefetch=2,
        grid=(n_heads // hb, len(qs)),
        in_specs=[pl.BlockSpec(memory_space=pltpu.SMEM),
                  pl.BlockSpec(memory_space=pltpu.SMEM),
                  pl.BlockSpec((4, HEAD_DIM), lambda h, p, qt, kt: (0, 0)),
                  pl.BlockSpec((1, W), lambda h, p, qt, kt: (0, 0)),
                  pl.BlockSpec((T, WB), lambda h, p, qt, kt: (qt[p], bq + h)),
                  pl.BlockSpec((T, WB), lambda h, p, qt, kt: (kt[p], bk + h)),
                  pl.BlockSpec((T, WB), lambda h, p, qt, kt: (kt[p], bv + h)),
                  pl.BlockSpec((hb, R, 2 * R), lambda h, p, qt, kt: (h, 0, 0))],
        out_specs=pl.BlockSpec((T, WB), lambda h, p, qt, kt: (qt[p], h)),
        scratch_shapes=[pltpu.VMEM((2 * hb, T, 128), F32), pltpu.VMEM((2 * hb, T, 128), F32),
                        pltpu.VMEM((2 * hb, T, W), F32)],
    )
    est = (2 * (4 * T * WB * 2 + hb * 2 * R * R * 4) + 2 * hb * T * W * 4 + 4 * hb * T * 128 * 4
           + 16 * R * T * 4)
    return pl.pallas_call(
        functools.partial(_attn_kernel, row_chunks=T // R),
        out_shape=jax.ShapeDtypeStruct((S, n_heads * W), BF16),
        grid_spec=grid_spec,
        compiler_params=_cparams(2, est),
        name="diff_attention",
    )(qt, kt, rel_bias, jnp.full((1,), lam_init, F32), diff_lambda, subln.reshape(1, W),
      z, z, z, bias_band)


def _ret_kernel(cd_ref, q_ref, k_ref, v_ref, g_ref, cos_ref, sin_ref, dm_ref, xi_ref, ze_ref,
                o_ref, R_ref):
    t = pl.program_id(0)
    n_heads = R_ref.shape[0]

    @pl.when(t == 0)
    def _():
        R_ref[...] = jnp.zeros(R_ref.shape, F32)

    half = HEAD_DIM // 2
    Rs = [R_ref[h] for h in range(n_heads)]
    for c in range(q_ref.shape[0] // CHUNK):
        sl = slice(c * CHUNK, (c + 1) * CHUNK)
        cs = cos_ref[sl, :]
        sn = sin_ref[sl, :]
        for h in range(n_heads):
            hc = slice(h * HEAD_DIM, (h + 1) * HEAD_DIM)
            q = q_ref[sl, hc].astype(F32)
            k = k_ref[sl, hc].astype(F32)
            v = v_ref[sl, hc]
            qr = q * cs + pltpu.roll(q, half, 1) * sn
            kr = (k * cs + pltpu.roll(k, half, 1) * sn) * (HEAD_DIM ** -0.5)
            qb = qr.astype(BF16)
            inner = lax.dot_general(qb, kr.astype(BF16), (((1,), (1,)), ((), ())),
                                    preferred_element_type=F32) * dm_ref[h]
            o = (jnp.dot(inner.astype(BF16), v, preferred_element_type=F32)
                 + jnp.dot(qb, Rs[h].astype(BF16), preferred_element_type=F32) * xi_ref[h])
            Rs[h] = Rs[h] * cd_ref[h] + lax.dot_general(
                (kr * ze_ref[h]).astype(BF16), v, (((0,), (0,)), ((), ())), preferred_element_type=F32)
            o = o * lax.rsqrt(jnp.mean(o * o, axis=-1, keepdims=True) + EPS)
            g = g_ref[sl, hc].astype(F32)
            o_ref[sl, hc] = (jax.nn.silu(g) * o).astype(o_ref.dtype)
    for h in range(n_heads):
        R_ref[h] = Rs[h]


def _retention_tables(S, n_heads):
    half = HEAD_DIM // 2
    pos = jnp.arange(S, dtype=F32)
    theta = 1.0 / (10000.0 ** jnp.linspace(0.0, 1.0, half, dtype=F32))
    ang = pos[:, None] * theta[None, :]
    cos = jnp.cos(ang)
    sin = jnp.sin(ang)
    cosf = jnp.concatenate([cos, cos], axis=-1)
    sinf = jnp.concatenate([-sin, sin], axis=-1)
    log_gamma = jnp.log1p(-jnp.exp2(-5.0 - jnp.arange(n_heads, dtype=F32)))
    idx = jnp.arange(CHUNK)
    diff = idx[:, None] - idx[None, :]
    dmat = jnp.where(diff >= 0,
                     jnp.exp(log_gamma[:, None, None] * jnp.maximum(diff, 0).astype(F32)), 0.0)
    xi = jnp.exp(log_gamma[:, None] * (idx[None, :] + 1).astype(F32))[..., None]
    zeta = jnp.exp(log_gamma[:, None] * (CHUNK - 1 - idx)[None, :].astype(F32))[..., None]
    chunk_decay = jnp.exp(log_gamma * CHUNK)
    return cosf, sinf, dmat, xi, zeta, chunk_decay


def _retention(z, tables, *, off_q, off_k, off_v, off_g, n_heads, TR):
    S = z.shape[0]
    cosf, sinf, dmat, xi, zeta, chunk_decay = tables
    WR = n_heads * HEAD_DIM
    assert all(o % WR == 0 for o in (off_q, off_k, off_v, off_g))
    blk = lambda off: pl.BlockSpec((TR, WR), lambda t: (t, off // WR))
    tab = pl.BlockSpec((TR, HEAD_DIM), lambda t: (t, 0))
    whole = lambda last: pl.BlockSpec((n_heads, CHUNK, last), lambda t: (0, 0, 0))
    est = (2 * (5 * TR * WR * 2 + 2 * TR * HEAD_DIM * 4 + n_heads * 3 * CHUNK * CHUNK * 4)
           + n_heads * 16 * CHUNK * CHUNK * 4)
    return pl.pallas_call(
        _ret_kernel,
        out_shape=jax.ShapeDtypeStruct((S, WR), BF16),
        grid=(S // TR,),
        in_specs=[pl.BlockSpec(memory_space=pltpu.SMEM),
                  blk(off_q), blk(off_k), blk(off_v), blk(off_g), tab, tab,
                  whole(CHUNK), whole(1), whole(1)],
        out_specs=pl.BlockSpec((TR, WR), lambda t: (t, 0)),
        scratch_shapes=[pltpu.VMEM((n_heads, HEAD_DIM, HEAD_DIM), F32)],
        compiler_params=_cparams(1, est),
        name="retention",
    )(chunk_decay, z, z, z, z, cosf, sinf, dmat, xi, zeta)


def _s5c_kernel(uf_ref, e_ref, f_ref, a_ref, apr_ref, api_ref, d_ref, o_ref, carry_ref):
    t = pl.program_id(1)
    Nc = uf_ref.shape[0]
    C = apr_ref.shape[2]

    @pl.when(t == 0)
    def _():
        carry_ref[...] = jnp.zeros(carry_ref.shape, F32)

    uf = uf_ref[...]
    x = jnp.dot(uf, e_ref[0], preferred_element_type=F32)
    ng = Nc // 8
    hr = x[:, :C].reshape(ng, 8, C)
    hi = x[:, C:].reshape(ng, 8, C)
    apr = apr_ref[0]
    api = api_ref[0]
    sub = lax.broadcasted_iota(jnp.int32, (ng, 8, C), 1)
    for d in (1, 2, 4):
        a_r = apr[d - 1:d, :].reshape(1, 1, C)
        a_i = api[d - 1:d, :].reshape(1, 1, C)
        sr = pltpu.roll(hr, d, 1)
        si = pltpu.roll(hi, d, 1)
        m = sub >= d
        nr = hr + jnp.where(m, a_r * sr - a_i * si, 0.0)
        ni = hi + jnp.where(m, a_r * si + a_i * sr, 0.0)
        hr, hi = nr, ni
    cin = carry_ref[...]
    c_r = cin[:, :C]
    c_i = cin[:, C:]
    out_r, out_i = [], []
    for g in range(ng):
        b_r = jnp.broadcast_to(c_r, (8, C))
        b_i = jnp.broadcast_to(c_i, (8, C))
        g_r = hr[g] + (apr * b_r - api * b_i)
        g_i = hi[g] + (apr * b_i + api * b_r)
        out_r.append(g_r)
        out_i.append(g_i)
        c_r = g_r[7:8, :]
        c_i = g_i[7:8, :]
    carry_ref[...] = jnp.concatenate([c_r, c_i], axis=1)
    h_end = jnp.concatenate([jnp.concatenate(out_r, axis=0), jnp.concatenate(out_i, axis=0)], axis=1)
    row = lax.broadcasted_iota(jnp.int32, (Nc, 2 * C), 0)
    h_in = jnp.where(row == 0, cin, pltpu.roll(h_end, 1, 0)).astype(BF16)
    y = (jnp.dot(h_in, f_ref[0], preferred_element_type=F32)
         + jnp.dot(uf, a_ref[0], preferred_element_type=F32) + d_ref[0] * uf.astype(F32))
    o_ref[...] = jax.nn.gelu(y).astype(o_ref.dtype)


def _s5c_params(lam_re, lam_im, log_dt, b_re, b_im, c_re, c_im, L):
    G, N = lam_re.shape
    sg = S5_SLICE_GROUPS
    nsl = G // sg
    J = S5_GROUP
    lr, li = lam_re.astype(F32), lam_im.astype(F32)
    dt = jnp.exp(log_dt.astype(F32))[:, None]
    mag = jnp.exp(lr * dt)
    ar = mag * jnp.cos(li * dt)
    ai = mag * jnp.sin(li * dt)
    den = lr * lr + li * li
    fr = ((ar - 1.0) * lr + ai * li) / den
    fi = (ai * lr - (ar - 1.0) * li) / den
    br_, bi_ = b_re.astype(F32), b_im.astype(F32)
    bbr = fr[..., None] * br_ - fi[..., None] * bi_
    bbi = fr[..., None] * bi_ + fi[..., None] * br_
    cr, ci = c_re.astype(F32), c_im.astype(F32)
    pr, pi = [jnp.ones_like(ar)], [jnp.zeros_like(ar)]
    for _ in range(L):
        pr.append(pr[-1] * ar - pi[-1] * ai)
        pi.append(pr[-2] * ai + pi[-1] * ar)
    P_r, P_i = jnp.stack(pr), jnp.stack(pi)
    ca_r = cr[None] * P_r[:, :, None, :] - ci[None] * P_i[:, :, None, :]
    ca_i = cr[None] * P_i[:, :, None, :] + ci[None] * P_r[:, :, None, :]
    K = (jnp.einsum('tgon,gni->tgoi', ca_r[:L], bbr) - jnp.einsum('tgon,gni->tgoi', ca_i[:L], bbi))
    lag = jnp.arange(L)[None, :] - jnp.arange(L)[:, None]
    eye = jnp.eye(sg, dtype=F32)
    A = jnp.where((lag >= 0)[:, :, None, None, None], K[jnp.clip(lag, 0, L - 1)], 0.0)
    A = jnp.einsum('stkgoi,gh->ksgitho', A.reshape(L, L, nsl, sg, J, J), eye).reshape(
        nsl, L * sg * J, L * sg * J)
    rev = jnp.stack(pr[L - 1::-1]), jnp.stack(pi[L - 1::-1])
    e_r = rev[0][:, :, :, None] * bbr[None] - rev[1][:, :, :, None] * bbi[None]
    e_i = rev[0][:, :, :, None] * bbi[None] + rev[1][:, :, :, None] * bbr[None]

    def enc(e):
        return jnp.einsum('skgni,gh->ksgihn', e.reshape(L, nsl, sg, N, J), eye).reshape(
            nsl, L * sg * J, sg * N)

    def dec(f):
        return jnp.einsum('tkgon,gh->khntgo', f.reshape(L, nsl, sg, J, N), eye).reshape(
            nsl, sg * N, L * sg * J)

    E = jnp.concatenate([enc(e_r), enc(e_i)], axis=-1).astype(BF16)
    Fm = jnp.concatenate([dec(ca_r[1:]), -dec(ca_i[1:])], axis=1).astype(BF16)
    qr, qi = [P_r[L]], [P_i[L]]
    for _ in range(7):
        qr.append(qr[-1] * P_r[L] - qi[-1] * P_i[L])
        qi.append(qr[-2] * P_i[L] + qi[-1] * P_r[L])
    apr = jnp.stack(qr).reshape(8, nsl, sg * N).transpose(1, 0, 2)
    api = jnp.stack(qi).reshape(8, nsl, sg * N).transpose(1, 0, 2)
    return E, Fm, A.astype(BF16), apr, api


def _s5c_scan(uf, s5p, d_skip, *, L, Nc):
    nchunks, width = uf.shape
    E, Fm, A, apr, api = s5p
    nsl, rows, xw = E.shape
    C = xw // 2
    dflat = jnp.tile(d_skip.astype(F32).reshape(nsl, 1, rows // L), (1, L, 1)).reshape(nsl, 1, rows)
    est = (2 * (2 * Nc * rows * 2 + 2 * rows * xw * 2 + rows * rows * 2 + 2 * 8 * C * 4 + rows * 4)
           + 10 * Nc * xw * 4 + 4 * Nc * rows * 4)
    return pl.pallas_call(
        _s5c_kernel,
        out_shape=jax.ShapeDtypeStruct((nchunks, width), BF16),
        grid=(nsl, nchunks // Nc),
        in_specs=[pl.BlockSpec((Nc, rows), lambda j, t: (t, j)),
                  pl.BlockSpec((1, rows, xw), lambda j, t: (j, 0, 0)),
                  pl.BlockSpec((1, xw, rows), lambda j, t: (j, 0, 0)),
                  pl.BlockSpec((1, rows, rows), lambda j, t: (j, 0, 0)),
                  pl.BlockSpec((1, 8, C), lambda j, t: (j, 0, 0)),
                  pl.BlockSpec((1, 8, C), lambda j, t: (j, 0, 0)),
                  pl.BlockSpec((1, 1, rows), lambda j, t: (j, 0, 0))],
        out_specs=pl.BlockSpec((Nc, rows), lambda j, t: (t, j)),
        scratch_shapes=[pltpu.VMEM((1, xw), F32)],
        compiler_params=_cparams(2, est),
        name="s5_chunk_scan",
    )(uf, E, Fm, A, apr, api, dflat)


def _glu_kernel(x_ref, w_ref, o_ref):
    zg = jnp.dot(x_ref[...], w_ref[...], preferred_element_type=F32)
    W = zg.shape[1] // 2
    o_ref[...] = (zg[:, :W] * jax.nn.sigmoid(zg[:, W:])).astype(o_ref.dtype)


def _glu(x, w, layer, *, bm):
    M, K = x.shape
    N2 = w.shape[2]
    est = 2 * (bm * K * 2 + K * N2 * 2 + bm * N2) + 3 * bm * N2 * 4
    return pl.pallas_call(
        _glu_kernel,
        out_shape=jax.ShapeDtypeStruct((M, N2 // 2), BF16),
        grid=(M // bm,),
        in_specs=[pl.BlockSpec((bm, K), lambda i: (i, 0)),
                  pl.BlockSpec((None, K, N2), lambda i: (layer, 0, 0))],
        out_specs=pl.BlockSpec((bm, N2 // 2), lambda i: (i, 0)),
        compiler_params=_cparams(1, est),
        name="s5_glu",
    )(x, w)


def _s5_chunked(z, s5p, d_skip, wglu, layer, *, off_u, width, L, Nc, bm):
    S = z.shape[0]
    sw = S5_SLICE_GROUPS * S5_GROUP
    nsl = width // sw
    u = z[:, off_u:off_u + width].reshape(S // L, L, nsl, sw)
    uf = u.transpose(0, 2, 1, 3).reshape(S // L, nsl * L * sw)
    yf = _s5c_scan(uf, s5p, d_skip, L=L, Nc=Nc)
    zc = yf.reshape(S // L, nsl, L, sw).transpose(0, 2, 1, 3).reshape(S, width)
    return _glu(zc, wglu, layer, bm=bm)


def _merge_kernel(gl_ref, od_ref, or_ref, os_ref, wg0_ref, wg1_ref, wg2_ref, wd_ref, wr_ref, ws_ref,
                  o_ref):
    gl = gl_ref[...]

    def branch(wg_ref, x_ref, w_ref):
        gate = jax.nn.sigmoid(jnp.dot(gl, wg_ref[...], preferred_element_type=F32))
        return gate * jnp.dot(x_ref[...], w_ref[...], preferred_element_type=F32)

    o_ref[...] = (branch(wg0_ref, od_ref, wd_ref) + branch(wg1_ref, or_ref, wr_ref)
                  + branch(wg2_ref, os_ref, ws_ref)).astype(o_ref.dtype)


def _merge(z, o_diff, o_ret, o_s5, wg, wd, wr, ws, layer, *, off_gate, bm, bn):
    S = z.shape[0]
    R = wg.shape[1]
    D = wd.shape[2]
    nb = D // bn
    kd, kr, ks = wd.shape[1], wr.shape[1], ws.shape[1]
    row = lambda k: pl.BlockSpec((bm, k), lambda i, j: (i, 0))
    col = lambda k: _wspec(layer, k, bn)
    est = 2 * (bm * (R + kd + kr + ks) * 2 + (3 * R + kd + kr + ks) * bn * 2 + bm * bn * 2) + 8 * bm * bn * 4
    return pl.pallas_call(
        _merge_kernel,
        out_shape=jax.ShapeDtypeStruct((S, D), BF16),
        grid=(S // bm, nb),
        in_specs=[pl.BlockSpec((bm, R), lambda i, j: (i, off_gate // R)),
                  row(kd), row(kr), row(ks),
                  _wspec(layer, R, bn),
                  _wspec(layer, R, bn, lambda j: nb + j),
                  _wspec(layer, R, bn, lambda j: 2 * nb + j),
                  col(kd), col(kr), col(ks)],
        out_specs=pl.BlockSpec((bm, bn), lambda i, j: (i, j)),
        compiler_params=_cparams(2, est),
        name="gated_merge",
    )(z, o_diff, o_ret, o_s5, wg, wg, wg, wd, wr, ws)


def _mm_res_kernel(x_ref, w_ref, h_ref, o_ref):
    o_ref[...] = h_ref[...] + jnp.dot(x_ref[...], w_ref[...], preferred_element_type=F32)


def _mm_res_norm_kernel(x_ref, w_ref, h_ref, g_ref, o_ref, hg_ref, ss_ref):
    hn = h_ref[...] + jnp.dot(x_ref[...], w_ref[...], preferred_element_type=F32)
    o_ref[...] = hn
    hg_ref[...] = (hn * g_ref[...]).astype(hg_ref.dtype)
    part = _lane_partial_sumsq(hn)

    @pl.when(pl.program_id(1) == 0)
    def _():
        ss_ref[...] = part

    @pl.when(pl.program_id(1) > 0)
    def _():
        ss_ref[...] += part


def _mm_res(x, w, layer, h, *, bm, bn, in_place, next_gain=None):
    M, K = x.shape
    N = w.shape[2]
    est = 2 * (bm * K * 2 + K * bn * 2 + 2 * bm * bn * 4 + bm * bn * 2 + bm * 512) + 2 * bm * bn * 4
    in_specs = [pl.BlockSpec((bm, K), lambda i, j: (i, 0)),
                _wspec(layer, K, bn),
                pl.BlockSpec((bm, bn), lambda i, j: (i, j))]
    h_spec = pl.BlockSpec((bm, bn), lambda i, j: (i, j))
    common = dict(grid=(M // bm, N // bn), input_output_aliases={2: 0} if in_place else {},
                  compiler_params=_cparams(2, est))
    if next_gain is None:
        return pl.pallas_call(
            _mm_res_kernel, out_shape=jax.ShapeDtypeStruct((M, N), F32), in_specs=in_specs,
            out_specs=h_spec, name="mm_residual", **common)(x, w, h)
    return pl.pallas_call(
        _mm_res_norm_kernel,
        out_shape=(jax.ShapeDtypeStruct((M, N), F32), jax.ShapeDtypeStruct((M, N), BF16),
                   jax.ShapeDtypeStruct((M, 128), F32)),
        in_specs=in_specs + [pl.BlockSpec((1, bn), lambda i, j: (0, j))],
        out_specs=(h_spec, pl.BlockSpec((bm, bn), lambda i, j: (i, j)),
                   pl.BlockSpec((bm, 128), lambda i, j: (i, 0))),
        name="mm_residual_norm", **common)(x, w, h, next_gain.reshape(1, N))


def _ffn_up_kernel(hg_ref, ss_ref, wg_ref, wu_ref, o_ref):
    hg = hg_ref[...]
    r = _row_rsqrt(ss_ref, hg_ref.shape[1])
    a = jnp.dot(hg, wg_ref[...], preferred_element_type=F32) * r
    b = jnp.dot(hg, wu_ref[...], preferred_element_type=F32) * r
    o_ref[...] = (jax.nn.silu(a) * b).astype(o_ref.dtype)


def _ffn_up(hg, ss, wg, wu, layer, *, bm, bn):
    M, K = hg.shape
    N = wg.shape[2]
    est = 2 * (bm * K * 2 + bm * 512 + 2 * K * bn * 2 + bm * bn * 2) + 5 * bm * bn * 4
    return pl.pallas_call(
        _ffn_up_kernel,
        out_shape=jax.ShapeDtypeStruct((M, N), BF16),
        grid=(M // bm, pl.cdiv(N, bn)),
        in_specs=[pl.BlockSpec((bm, K), lambda i, j: (i, 0)),
                  pl.BlockSpec((bm, 128), lambda i, j: (i, 0)),
                  _wspec(layer, K, bn),
                  _wspec(layer, K, bn)],
        out_specs=pl.BlockSpec((bm, bn), lambda i, j: (i, j)),
        compiler_params=_cparams(2, est),
        name="ffn_up",
    )(hg, ss, wg, wu)


def _ple_update(h_ref, g_ref, p_ref, wple_ref, wgd_ref, wgu_ref, hp_ref):
    _rms_rows(h_ref, g_ref, hp_ref, min(64, h_ref.shape[0]))
    t = jnp.dot(hp_ref[...], wgd_ref[...], preferred_element_type=F32).astype(BF16)
    gate = jax.nn.sigmoid(jnp.dot(t, wgu_ref[...], preferred_element_type=F32))
    e = jnp.dot(p_ref[...].astype(BF16), wple_ref[...], preferred_element_type=F32)
    return h_ref[...] + e * gate


def _ple_kernel(h_ref, g_ref, p_ref, wple_ref, wgd_ref, wgu_ref, gn_ref, o_ref, hg_ref, ss_ref, hp_ref):
    hn = _ple_update(h_ref, g_ref, p_ref, wple_ref, wgd_ref, wgu_ref, hp_ref)
    o_ref[...] = hn
    hg_ref[...] = (hn * gn_ref[...]).astype(hg_ref.dtype)
    ss_ref[...] = _lane_partial_sumsq(hn)


def _ple_final_kernel(h_ref, g_ref, p_ref, wple_ref, wgd_ref, wgu_ref, gn_ref, o_ref, hp_ref):
    hn = _ple_update(h_ref, g_ref, p_ref, wple_ref, wgd_ref, wgu_ref, hp_ref)
    ms = jnp.mean(hn * hn, axis=-1, keepdims=True)
    o_ref[...] = hn * lax.rsqrt(ms + EPS) * gn_ref[...]


def _ple(h, gain, p, wple, wgd, wgu, layer, next_gain, *, bm, final):
    M, D = h.shape
    P = p.shape[2]
    R = wgd.shape[2]
    est = 2 * (2 * bm * D * 4 + bm * D * 2 + bm * P * 4 + (P + 2 * R) * D * 2) + bm * D * 2 + 4 * bm * D * 4
    row = pl.BlockSpec((bm, D), lambda i: (i, 0))
    in_specs = [row,
                pl.BlockSpec((1, D), lambda i: (0, 0)),
                pl.BlockSpec((None, bm, P), lambda i: (layer, i, 0)),
                pl.BlockSpec((None, P, D), lambda i: (layer, 0, 0)),
                pl.BlockSpec((None, D, R), lambda i: (layer, 0, 0)),
                pl.BlockSpec((None, R, D), lambda i: (layer, 0, 0)),
                pl.BlockSpec((1, D), lambda i: (0, 0))]
    common = dict(grid=(M // bm,), in_specs=in_specs, scratch_shapes=[pltpu.VMEM((bm, D), BF16)],
                  input_output_aliases={0: 0}, compiler_params=_cparams(1, est))
    args = (h, gain.reshape(1, D), p, wple, wgd, wgu, next_gain.reshape(1, D))
    if final:
        return pl.pallas_call(_ple_final_kernel, out_shape=jax.ShapeDtypeStruct((M, D), F32),
                              out_specs=row, name="ple_final", **common)(*args)
    return pl.pallas_call(
        _ple_kernel,
        out_shape=(jax.ShapeDtypeStruct((M, D), F32), jax.ShapeDtypeStruct((M, D), BF16),
                   jax.ShapeDtypeStruct((M, 128), F32)),
        out_specs=(row, row, pl.BlockSpec((bm, 128), lambda i: (i, 0))),
        name="ple", **common)(*args)


def kernel(x, p, rel_bias, norm_mix, w_in, diff_lambda, diff_subln, s5_lambda_re, s5_lambda_im, s5_log_dt, s5_b_re, s5_b_im, s5_c_re, s5_c_im, s5_d, s5_w_glu, w_gate_up, w_br_diff, w_br_ret, w_br_s5, w_o, norm_ffn, w_ffn_gate, w_ffn_up, w_ffn_down, norm_ple, w_ple, w_ple_gate_down, w_ple_gate_up, norm_final):
    B, S, D = x.shape
    assert B == 1
    depth = w_in.shape[0]
    diff_w, ret_w, s5_w = w_br_diff.shape[1], w_br_ret.shape[1], w_br_s5.shape[1]
    gate_rank = w_gate_up.shape[1]
    diff_heads = diff_w // (2 * HEAD_DIM)
    ret_heads = ret_w // HEAD_DIM
    off_dq = 0
    off_dk = off_dq + diff_w
    off_dv = off_dk + diff_w
    off_rq = off_dv + diff_w
    off_rk = off_rq + ret_w
    off_rv = off_rk + ret_w
    off_rg = off_rv + ret_w
    off_su = off_rg + ret_w
    off_gate = off_su + s5_w
    in_w = off_gate + gate_rank
    assert w_in.shape[2] == in_w
    hidden = w_ffn_gate.shape[2]

    T_attn = _tile(S, 1024)
    T_ret = _tile(S, 256)
    bm_in = _tile(S, 1024)
    bn_in = _tile(in_w, 768) if in_w % 768 == 0 else _tile(in_w, 512)
    bm_mg, bn_mg = _tile(S, 1024), _tile(D, 512)
    bm_o, bn_o = _tile(S, 1024), _tile(D, 512)
    bm_up, bn_up = _tile(S, 1024), min(hidden, 512)
    bm_dn, bn_dn = _tile(S, 512), _tile(D, 512)
    bm_ple = _tile(S, 256)

    h = x.reshape(S, D)
    hg, ss = _row_prep(h, norm_mix[0], bm=bm_ple)
    bias_band = _bias_band(rel_bias, diff_heads, min(ATTN_ROW_CHUNK, T_attn))
    ret_tables = _retention_tables(S, ret_heads)
    z_scale = jnp.where(jnp.arange(in_w) < off_dk, HEAD_DIM ** -0.5 * LOG2E, 1.0).astype(F32)

    (w_in_b, w_glu_b, w_gate_b, w_brd_b, w_brr_b, w_brs_b, w_o_b, w_fg_b, w_fu_b, w_fd_b, w_ple_b,
     w_pgd_b, w_pgu_b) = (w.astype(BF16) for w in (
         w_in, s5_w_glu, w_gate_up, w_br_diff, w_br_ret, w_br_s5, w_o, w_ffn_gate, w_ffn_up,
         w_ffn_down, w_ple, w_ple_gate_down, w_ple_gate_up))
    p3 = p.reshape(depth, S, p.shape[-1])

    for i in range(depth):
        lam_init = 0.8 - 0.6 * math.exp(-0.3 * i)
        z = _norm_mm(hg, ss, w_in_b, i, z_scale, bm=bm_in, bn=bn_in)
        o_diff = _diff_attention(z, rel_bias, bias_band, diff_lambda[i], diff_subln[i], lam_init,
                                 off_q=off_dq, off_k=off_dk, off_v=off_dv, n_heads=diff_heads, T=T_attn)
        o_ret = _retention(z, ret_tables, off_q=off_rq, off_k=off_rk, off_v=off_rv, off_g=off_rg,
                           n_heads=ret_heads, TR=T_ret)
        s5p = _s5c_params(s5_lambda_re[i], s5_lambda_im[i], s5_log_dt[i], s5_b_re[i], s5_b_im[i],
                          s5_c_re[i], s5_c_im[i], S5_CHUNK)
        o_s5 = _s5_chunked(z, s5p, s5_d[i], w_glu_b, i, off_u=off_su, width=s5_w, L=S5_CHUNK,
                           Nc=_tile(S // S5_CHUNK, 256), bm=_tile(S, 1024))
        mixed = _merge(z, o_diff, o_ret, o_s5, w_gate_b, w_brd_b, w_brr_b, w_brs_b, i,
                       off_gate=off_gate, bm=bm_mg, bn=bn_mg)
        h, hg, ss = _mm_res(mixed, w_o_b, i, h, bm=bm_o, bn=bn_o, in_place=i > 0, next_gain=norm_ffn[i])
        act = _ffn_up(hg, ss, w_fg_b, w_fu_b, i, bm=bm_up, bn=bn_up)
        h = _mm_res(act, w_fd_b, i, h, bm=bm_dn, bn=bn_dn, in_place=True)
        if i + 1 < depth:
            h, hg, ss = _ple(h, norm_ple[i], p3, w_ple_b, w_pgd_b, w_pgu_b, i, norm_mix[i + 1],
                             bm=bm_ple, final=False)
        else:
            out = _ple(h, norm_ple[i], p3, w_ple_b, w_pgd_b, w_pgu_b, i, norm_final,
                       bm=bm_ple, final=True)
    return out.reshape(B, S, D)
```

```python
import functools
import math

import jax
import jax.numpy as jnp
from jax import lax
from jax.experimental import pallas as pl
from jax.experimental.pallas import tpu as pltpu

F32 = jnp.float32
BF16 = jnp.bfloat16
EPS = 1e-6

HEAD_DIM = 128
CHUNK = 128
REL_BUCKETS = 32
REL_MAX_DIST = 128
S5_GROUP = 16
S5_STATE = 64
S5_SLICE_GROUPS = 8
S5_CHUNK = 16
MASK_VALUE = -1e30
LOG2E = math.log2(math.e)
ATTN_ROW_CHUNK = 128
ATTN_HEADS_PER_STEP = 2

V7X_VMEM_BYTES = 64 * 1024 * 1024
V7X_VMEM_REQUEST_CAP = 60 * 1024 * 1024
SPILL_AND_TEMP_ALLOWANCE = 8 * 1024 * 1024


def _cparams(n_axes, est_bytes):
    limit = int(min(est_bytes + SPILL_AND_TEMP_ALLOWANCE, V7X_VMEM_REQUEST_CAP))
    return pltpu.CompilerParams(dimension_semantics=("arbitrary",) * n_axes, vmem_limit_bytes=limit)


def _tile(n, want):
    t = min(n, want)
    while n % t:
        t //= 2
    return t


def _rms_rows(x_ref, g_ref, dst_ref, rows):
    n = x_ref.shape[0] // rows

    def body(i, c):
        r = pl.multiple_of(i * rows, rows)
        x = x_ref[pl.ds(r, rows), :]
        ms = jnp.mean(x * x, axis=-1, keepdims=True)
        dst_ref[pl.ds(r, rows), :] = (x * lax.rsqrt(ms + EPS) * g_ref[...]).astype(dst_ref.dtype)
        return c

    lax.fori_loop(0, n, body, 0)


def _lane_partial_sumsq(x):
    sq = x * x
    part = sq[:, 0:128]
    for j in range(1, x.shape[1] // 128):
        part = part + sq[:, j * 128:(j + 1) * 128]
    return part


def _row_rsqrt(ss_ref, n_cols):
    return lax.rsqrt(jnp.sum(ss_ref[...], axis=-1, keepdims=True) * (1.0 / n_cols) + EPS)


def _row_prep_kernel(x_ref, g_ref, hg_ref, ss_ref):
    x = x_ref[...]
    hg_ref[...] = (x * g_ref[...]).astype(hg_ref.dtype)
    ss_ref[...] = _lane_partial_sumsq(x)


def _row_prep(x, gain, *, bm):
    M, D = x.shape
    return pl.pallas_call(
        _row_prep_kernel,
        out_shape=(jax.ShapeDtypeStruct((M, D), BF16), jax.ShapeDtypeStruct((M, 128), F32)),
        grid=(M // bm,),
        in_specs=[pl.BlockSpec((bm, D), lambda i: (i, 0)), pl.BlockSpec((1, D), lambda i: (0, 0))],
        out_specs=(pl.BlockSpec((bm, D), lambda i: (i, 0)), pl.BlockSpec((bm, 128), lambda i: (i, 0))),
        compiler_params=_cparams(1, 2 * (bm * D * 6 + bm * 512) + 2 * bm * D * 4),
        name="row_prep",
    )(x, gain.reshape(1, D))


def _norm_mm_kernel(hg_ref, ss_ref, w_ref, cs_ref, o_ref):
    acc = jnp.dot(hg_ref[...], w_ref[...], preferred_element_type=F32)
    r = _row_rsqrt(ss_ref, hg_ref.shape[1])
    o_ref[...] = ((acc * r) * cs_ref[...]).astype(o_ref.dtype)


def _wspec(layer, k, bn, col_block=lambda j: j):
    return pl.BlockSpec((None, k, bn), lambda i, j: (layer, 0, col_block(j)))


def _norm_mm(hg, ss, w, layer, col_scale, *, bm, bn, out_dtype=BF16):
    M, K = hg.shape
    N = w.shape[2]
    est = 2 * (bm * K * 2 + bm * 512 + K * bn * 2 + bm * bn * 2) + 4 * bm * bn * 4
    return pl.pallas_call(
        _norm_mm_kernel,
        out_shape=jax.ShapeDtypeStruct((M, N), out_dtype),
        grid=(M // bm, N // bn),
        in_specs=[pl.BlockSpec((bm, K), lambda i, j: (i, 0)),
                  pl.BlockSpec((bm, 128), lambda i, j: (i, 0)),
                  _wspec(layer, K, bn),
                  pl.BlockSpec((1, bn), lambda i, j: (0, j))],
        out_specs=pl.BlockSpec((bm, bn), lambda i, j: (i, j)),
        compiler_params=_cparams(2, est),
        name="norm_mm",
    )(hg, ss, w, col_scale.reshape(1, N))


def _bias_band_kernel(rb_ref, o_ref):
    h = pl.program_id(0)
    R = o_ref.shape[1]
    r = lax.broadcasted_iota(jnp.int32, (R, 2 * R), 0)
    c = lax.broadcasted_iota(jnp.int32, (R, 2 * R), 1)
    rel = R + r - c
    n = jnp.maximum(rel, 0)
    max_exact = REL_BUCKETS // 2
    nf = jnp.maximum(n, 1).astype(F32)
    large = max_exact + (jnp.log(nf / max_exact) / math.log(REL_MAX_DIST / max_exact)
                         * (REL_BUCKETS - max_exact)).astype(jnp.int32)
    large = jnp.minimum(large, REL_BUCKETS - 1)
    bucket = jnp.where(n < max_exact, n, large)
    val = jnp.zeros((R, 2 * R), F32)
    for b in range(REL_BUCKETS):
        val = jnp.where(bucket == b, rb_ref[b, h], val)
    far = rb_ref[REL_BUCKETS - 1, h]
    o_ref[0] = jnp.where(rel >= 0, (val - far) * LOG2E, MASK_VALUE)


def _bias_band(rel_bias, n_heads, R):
    assert R >= REL_MAX_DIST
    return pl.pallas_call(
        _bias_band_kernel,
        out_shape=jax.ShapeDtypeStruct((n_heads, R, 2 * R), F32),
        grid=(n_heads,),
        in_specs=[pl.BlockSpec(memory_space=pltpu.SMEM)],
        out_specs=pl.BlockSpec((1, R, 2 * R), lambda h: (h, 0, 0)),
        compiler_params=_cparams(1, 24 * R * R * 4),
        name="t5_bias_band",
    )(rel_bias)


def _attn_kernel(qt_ref, kt_ref, rb_ref, li_ref, dl_ref, sub_ref, q_ref, k_ref, v_ref, bt_ref,
                 o_ref, m_ref, l_ref, acc_ref, *, row_chunks):
    hg = pl.program_id(0)
    p = pl.program_id(1)
    qi = qt_ref[p]
    ki = kt_ref[p]
    T = q_ref.shape[0]
    W = 2 * HEAD_DIM
    hb = q_ref.shape[1] // W

    @pl.when(ki == 0)
    def _():
        m_ref[...] = jnp.full(m_ref.shape, MASK_VALUE, F32)
        l_ref[...] = jnp.zeros(l_ref.shape, F32)
        acc_ref[...] = jnp.zeros(acc_ref.shape, F32)

    R = T // row_chunks
    nb = R // 128

    def step(kind):
        for r in range(row_chunks):
            rows = slice(r * R, (r + 1) * R)
            ncol = (r + 1) * R if kind == "diag" else T
            ncb = ncol // 128
            if kind == "diag":
                band0 = ncb - 2 * nb
            elif kind == "near" and r == 0:
                band0 = ncb - nb
            else:
                band0 = None
            for hh in range(hb):
                v = v_ref[0:ncol, hh * W:(hh + 1) * W]
                far = rb_ref[REL_BUCKETS - 1, hg * hb + hh] * LOG2E
                for c in range(2):
                    pc = 2 * hh + c
                    col0 = hh * W + c * HEAD_DIM
                    qc = q_ref[rows, col0:col0 + HEAD_DIM]
                    kc = k_ref[0:ncol, col0:col0 + HEAD_DIM]
                    s = lax.dot_general(qc, kc, (((1,), (1,)), ((), ())), preferred_element_type=F32)
                    blocks = []
                    for j in range(ncb):
                        sj = s[:, j * 128:(j + 1) * 128]
                        if band0 is not None and j - band0 >= 0:
                            jb = j - band0
                            sj = sj + bt_ref[hh, :, jb * 128:(jb + 1) * 128]
                        blocks.append(sj)
                    blk_max = blocks[0]
                    for sj in blocks[1:]:
                        blk_max = jnp.maximum(blk_max, sj)
                    m_prev = m_ref[pc, rows, :]
                    m_new = jnp.maximum(m_prev, jnp.max(blk_max, axis=-1, keepdims=True) + far)
                    alpha = jnp.exp2(m_prev - m_new)
                    shift = m_new - far
                    l_part = alpha * l_ref[pc, rows, :]
                    ps = []
                    for sj in blocks:
                        pj = jnp.exp2(sj - shift)
                        l_part = l_part + pj
                        ps.append(pj.astype(BF16))
                    l_ref[pc, rows, :] = l_part
                    m_ref[pc, rows, :] = m_new
                    pv = jnp.dot(jnp.concatenate(ps, axis=1), v, preferred_element_type=F32)
                    acc_ref[pc, rows, :] = (jnp.concatenate([alpha] * (W // 128), axis=1)
                                            * acc_ref[pc, rows, :] + pv)

    @pl.when(qi - ki >= 2)
    def _():
        step("far")

    @pl.when(qi - ki == 1)
    def _():
        step("near")

    @pl.when(qi == ki)
    def _():
        step("diag")
        lam_init = li_ref[0]
        dl = dl_ref[...]
        lam = (jnp.exp(jnp.sum(dl[0:1] * dl[1:2], axis=-1, keepdims=True))
               - jnp.exp(jnp.sum(dl[2:3] * dl[3:4], axis=-1, keepdims=True)) + lam_init)
        for hh in range(hb):
            l0 = jnp.sum(l_ref[2 * hh], axis=-1, keepdims=True)
            l1 = jnp.sum(l_ref[2 * hh + 1], axis=-1, keepdims=True)
            o = acc_ref[2 * hh] / l0 - lam * (acc_ref[2 * hh + 1] / l1)
            ms = jnp.mean(o * o, axis=-1, keepdims=True)
            y = (o * lax.rsqrt(ms + EPS) * sub_ref[...]) * (1.0 - lam_init)
            o_ref[:, hh * W:(hh + 1) * W] = y.astype(o_ref.dtype)


def _diff_attention(z, rel_bias, bias_band, diff_lambda, subln, lam_init, *, off_q, off_k, off_v,
                    n_heads, T):
    S = z.shape[0]
    W = 2 * HEAD_DIM
    R = bias_band.shape[1]
    nq = S // T
    qs, ks = [], []
    for qi in range(nq):
        for ki in range(qi + 1):
            qs.append(qi)
            ks.append(ki)
    qt = jnp.asarray(qs, jnp.int32)
    kt = jnp.asarray(ks, jnp.int32)
    hb = ATTN_HEADS_PER_STEP if n_heads % ATTN_HEADS_PER_STEP == 0 else 1
    WB = hb * W
    assert off_q % WB == 0 and off_k % WB == 0 and off_v % WB == 0
    bq, bk, bv = off_q // WB, off_k // WB, off_v // WB
    grid_spec = pltpu.PrefetchScalarGridSpec(
        num_scalar_prefetch=2,
        grid=(n_heads // hb, len(qs)),
        in_specs=[pl.BlockSpec(memory_space=pltpu.SMEM),
                  pl.BlockSpec(memory_space=pltpu.SMEM),
                  pl.BlockSpec((4, HEAD_DIM), lambda h, p, qt, kt: (0, 0)),
                  pl.BlockSpec((1, W), lambda h, p, qt, kt: (0, 0)),
                  pl.BlockSpec((T, WB), lambda h, p, qt, kt: (qt[p], bq + h)),
                  pl.BlockSpec((T, WB), lambda h, p, qt, kt: (kt[p], bk + h)),
                  pl.BlockSpec((T, WB), lambda h, p, qt, kt: (kt[p], bv + h)),
                  pl.BlockSpec((hb, R, 2 * R), lambda h, p, qt, kt: (h, 0, 0))],
        out_specs=pl.BlockSpec((T, WB), lambda h, p, qt, kt: (qt[p], h)),
        scratch_shapes=[pltpu.VMEM((2 * hb, T, 128), F32), pltpu.VMEM((2 * hb, T, 128), F32),
                        pltpu.VMEM((2 * hb, T, W), F32)],
    )
    est = (2 * (4 * T * WB * 2 + hb * 2 * R * R * 4) + 2 * hb * T * W * 4 + 4 * hb * T * 128 * 4
           + 16 * R * T * 4)
    return pl.pallas_call(
        functools.partial(_attn_kernel, row_chunks=T // R),
        out_shape=jax.ShapeDtypeStruct((S, n_heads * W), BF16),
        grid_spec=grid_spec,
        compiler_params=_cparams(2, est),
        name="diff_attention",
    )(qt, kt, rel_bias, jnp.full((1,), lam_init, F32), diff_lambda, subln.reshape(1, W),
      z, z, z, bias_band)


def _ret_kernel(cd_ref, q_ref, k_ref, v_ref, g_ref, cos_ref, sin_ref, dm_ref, xi_ref, ze_ref,
                o_ref, R_ref):
    t = pl.program_id(0)
    n_heads = R_ref.shape[0]

    @pl.when(t == 0)
    def _():
        R_ref[...] = jnp.zeros(R_ref.shape, F32)

    half = HEAD_DIM // 2
    Rs = [R_ref[h] for h in range(n_heads)]
    for c in range(q_ref.shape[0] // CHUNK):
        sl = slice(c * CHUNK, (c + 1) * CHUNK)
        cs = cos_ref[sl, :]
        sn = sin_ref[sl, :]
        for h in range(n_heads):
            hc = slice(h * HEAD_DIM, (h + 1) * HEAD_DIM)
            q = q_ref[sl, hc].astype(F32)
            k = k_ref[sl, hc].astype(F32)
            v = v_ref[sl, hc]
            qr = q * cs + pltpu.roll(q, half, 1) * sn
            kr = (k * cs + pltpu.roll(k, half, 1) * sn) * (HEAD_DIM ** -0.5)
            qb = qr.astype(BF16)
            inner = lax.dot_general(qb, kr.astype(BF16), (((1,), (1,)), ((), ())),
                                    preferred_element_type=F32) * dm_ref[h]
            o = (jnp.dot(inner.astype(BF16), v, preferred_element_type=F32)
                 + jnp.dot(qb, Rs[h].astype(BF16), preferred_element_type=F32) * xi_ref[h])
            Rs[h] = Rs[h] * cd_ref[h] + lax.dot_general(
                (kr * ze_ref[h]).astype(BF16), v, (((0,), (0,)), ((), ())), preferred_element_type=F32)
            o = o * lax.rsqrt(jnp.mean(o * o, axis=-1, keepdims=True) + EPS)
            g = g_ref[sl, hc].astype(F32)
            o_ref[sl, hc] = (jax.nn.silu(g) * o).astype(o_ref.dtype)
    for h in range(n_heads):
        R_ref[h] = Rs[h]


def _retention_tables(S, n_heads):
    half = HEAD_DIM // 2
    pos = jnp.arange(S, dtype=F32)
    theta = 1.0 / (10000.0 ** jnp.linspace(0.0, 1.0, half, dtype=F32))
    ang = pos[:, None] * theta[None, :]
    cos = jnp.cos(ang)
    sin = jnp.sin(ang)
    cosf = jnp.concatenate([cos, cos], axis=-1)
    sinf = jnp.concatenate([-sin, sin], axis=-1)
    log_gamma = jnp.log1p(-jnp.exp2(-5.0 - jnp.arange(n_heads, dtype=F32)))
    idx = jnp.arange(CHUNK)
    diff = idx[:, None] - idx[None, :]
    dmat = jnp.where(diff >= 0,
                     jnp.exp(log_gamma[:, None, None] * jnp.maximum(diff, 0).astype(F32)), 0.0)
    xi = jnp.exp(log_gamma[:, None] * (idx[None, :] + 1).astype(F32))[..., None]
    zeta = jnp.exp(log_gamma[:, None] * (CHUNK - 1 - idx)[None, :].astype(F32))[..., None]
    chunk_decay = jnp.exp(log_gamma * CHUNK)
    return cosf, sinf, dmat, xi, zeta, chunk_decay


def _retention(z, tables, *, off_q, off_k, off_v, off_g, n_heads, TR):
    S = z.shape[0]
    cosf, sinf, dmat, xi, zeta, chunk_decay = tables
    WR = n_heads * HEAD_DIM
    assert all(o % WR == 0 for o in (off_q, off_k, off_v, off_g))
    blk = lambda off: pl.BlockSpec((TR, WR), lambda t: (t, off // WR))
    tab = pl.BlockSpec((TR, HEAD_DIM), lambda t: (t, 0))
    whole = lambda last: pl.BlockSpec((n_heads, CHUNK, last), lambda t: (0, 0, 0))
    est = (2 * (5 * TR * WR * 2 + 2 * TR * HEAD_DIM * 4 + n_heads * 3 * CHUNK * CHUNK * 4)
           + n_heads * 16 * CHUNK * CHUNK * 4)
    return pl.pallas_call(
        _ret_kernel,
        out_shape=jax.ShapeDtypeStruct((S, WR), BF16),
        grid=(S // TR,),
        in_specs=[pl.BlockSpec(memory_space=pltpu.SMEM),
                  blk(off_q), blk(off_k), blk(off_v), blk(off_g), tab, tab,
                  whole(CHUNK), whole(1), whole(1)],
        out_specs=pl.BlockSpec((TR, WR), lambda t: (t, 0)),
        scratch_shapes=[pltpu.VMEM((n_heads, HEAD_DIM, HEAD_DIM), F32)],
        compiler_params=_cparams(1, est),
        name="retention",
    )(chunk_decay, z, z, z, z, cosf, sinf, dmat, xi, zeta)


def _s5c_kernel(uf_ref, e_ref, f_ref, a_ref, apr_ref, api_ref, d_ref, o_ref, carry_ref):
    t = pl.program_id(1)
    Nc = uf_ref.shape[0]
    C = apr_ref.shape[2]

    @pl.when(t == 0)
    def _():
        carry_ref[...] = jnp.zeros(carry_ref.shape, F32)

    uf = uf_ref[...]
    x = jnp.dot(uf, e_ref[0], preferred_element_type=F32)
    ng = Nc // 8
    hr = x[:, :C].reshape(ng, 8, C)
    hi = x[:, C:].reshape(ng, 8, C)
    apr = apr_ref[0]
    api = api_ref[0]
    sub = lax.broadcasted_iota(jnp.int32, (ng, 8, C), 1)
    for d in (1, 2, 4):
        a_r = apr[d - 1:d, :].reshape(1, 1, C)
        a_i = api[d - 1:d, :].reshape(1, 1, C)
        sr = pltpu.roll(hr, d, 1)
        si = pltpu.roll(hi, d, 1)
        m = sub >= d
        nr = hr + jnp.where(m, a_r * sr - a_i * si, 0.0)
        ni = hi + jnp.where(m, a_r * si + a_i * sr, 0.0)
        hr, hi = nr, ni
    cin = carry_ref[...]
    c_r = cin[:, :C]
    c_i = cin[:, C:]
    out_r, out_i = [], []
    for g in range(ng):
        b_r = jnp.broadcast_to(c_r, (8, C))
        b_i = jnp.broadcast_to(c_i, (8, C))
        g_r = hr[g] + (apr * b_r - api * b_i)
        g_i = hi[g] + (apr * b_i + api * b_r)
        out_r.append(g_r)
        out_i.append(g_i)
        c_r = g_r[7:8, :]
        c_i = g_i[7:8, :]
    carry_ref[...] = jnp.concatenate([c_r, c_i], axis=1)
    h_end = jnp.concatenate([jnp.concatenate(out_r, axis=0), jnp.concatenate(out_i, axis=0)], axis=1)
    row = lax.broadcasted_iota(jnp.int32, (Nc, 2 * C), 0)
    h_in = jnp.where(row == 0, cin, pltpu.roll(h_end, 1, 0)).astype(BF16)
    y = (jnp.dot(h_in, f_ref[0], preferred_element_type=F32)
         + jnp.dot(uf, a_ref[0], preferred_element_type=F32) + d_ref[0] * uf.astype(F32))
    o_ref[...] = jax.nn.gelu(y).astype(o_ref.dtype)


def _s5c_params(lam_re, lam_im, log_dt, b_re, b_im, c_re, c_im, L):
    G, N = lam_re.shape
    sg = S5_SLICE_GROUPS
    nsl = G // sg
    J = S5_GROUP
    lr, li = lam_re.astype(F32), lam_im.astype(F32)
    dt = jnp.exp(log_dt.astype(F32))[:, None]
    mag = jnp.exp(lr * dt)
    ar = mag * jnp.cos(li * dt)
    ai = mag * jnp.sin(li * dt)
    den = lr * lr + li * li
    fr = ((ar - 1.0) * lr + ai * li) / den
    fi = (ai * lr - (ar - 1.0) * li) / den
    br_, bi_ = b_re.astype(F32), b_im.astype(F32)
    bbr = fr[..., None] * br_ - fi[..., None] * bi_
    bbi = fr[..., None] * bi_ + fi[..., None] * br_
    cr, ci = c_re.astype(F32), c_im.astype(F32)
    pr, pi = [jnp.ones_like(ar)], [jnp.zeros_like(ar)]
    for _ in range(L):
        pr.append(pr[-1] * ar - pi[-1] * ai)
        pi.append(pr[-2] * ai + pi[-1] * ar)
    P_r, P_i = jnp.stack(pr), jnp.stack(pi)
    ca_r = cr[None] * P_r[:, :, None, :] - ci[None] * P_i[:, :, None, :]
    ca_i = cr[None] * P_i[:, :, None, :] + ci[None] * P_r[:, :, None, :]
    K = (jnp.einsum('tgon,gni->tgoi', ca_r[:L], bbr) - jnp.einsum('tgon,gni->tgoi', ca_i[:L], bbi))
    sw = sg * J
    eye = jnp.eye(sg, dtype=F32)
    lane_group = jnp.arange(sw) // J
    bd = jnp.einsum('tkgoi,gh->ktgiho', K.reshape(L, nsl, sg, J, J), eye).reshape(nsl, L, sw, sw)
    rows_a = [jnp.pad(bd[:, :L - s], ((0, 0), (s, 0), (0, 0), (0, 0))) for s in range(L)]
    A = jnp.stack(rows_a, axis=1)
    A = A.transpose(0, 1, 3, 2, 4).reshape(nsl, L * sw, L * sw)
    rev = jnp.stack(pr[L - 1::-1]), jnp.stack(pi[L - 1::-1])
    e_r = rev[0][:, :, :, None] * bbr[None] - rev[1][:, :, :, None] * bbi[None]
    e_i = rev[0][:, :, :, None] * bbi[None] + rev[1][:, :, :, None] * bbr[None]

    def enc(e):
        e4 = e.reshape(L, nsl, sg, N, J).transpose(1, 0, 2, 4, 3).reshape(nsl, L, sw, N)
        same = lane_group[:, None] == (jnp.arange(sg * N) // N)[None, :]
        return jnp.where(same, jnp.tile(e4, (1, 1, 1, sg)), 0.0).reshape(nsl, L * sw, sg * N)

    def dec(f):
        f4 = f.reshape(L, nsl, sg, J, N).transpose(1, 4, 0, 2, 3).reshape(nsl, N, L, sw)
        same = jnp.arange(sg)[:, None, None, None] == lane_group[None, None, None, :]
        return jnp.where(same, f4[:, None], 0.0).reshape(nsl, sg * N, L * sw)

    E = jnp.concatenate([enc(e_r), enc(e_i)], axis=-1).astype(BF16)
    Fm = jnp.concatenate([dec(ca_r[1:]), -dec(ca_i[1:])], axis=1).astype(BF16)
    qr, qi = [P_r[L]], [P_i[L]]
    for _ in range(7):
        qr.append(qr[-1] * P_r[L] - qi[-1] * P_i[L])
        qi.append(qr[-2] * P_i[L] + qi[-1] * P_r[L])
    apr = jnp.stack(qr).reshape(8, nsl, sg * N).transpose(1, 0, 2)
    api = jnp.stack(qi).reshape(8, nsl, sg * N).transpose(1, 0, 2)
    return E, Fm, A.astype(BF16), apr, api


def _s5c_scan(uf, s5p, d_skip, *, L, Nc):
    nchunks, width = uf.shape
    E, Fm, A, apr, api = s5p
    nsl, rows, xw = E.shape
    C = xw // 2
    dflat = jnp.tile(d_skip.astype(F32).reshape(nsl, 1, rows // L), (1, L, 1)).reshape(nsl, 1, rows)
    est = (2 * (2 * Nc * rows * 2 + 2 * rows * xw * 2 + rows * rows * 2 + 2 * 8 * C * 4 + rows * 4)
           + 10 * Nc * xw * 4 + 4 * Nc * rows * 4)
    return pl.pallas_call(
        _s5c_kernel,
        out_shape=jax.ShapeDtypeStruct((nchunks, width), BF16),
        grid=(nsl, nchunks // Nc),
        in_specs=[pl.BlockSpec((Nc, rows), lambda j, t: (t, j)),
                  pl.BlockSpec((1, rows, xw), lambda j, t: (j, 0, 0)),
                  pl.BlockSpec((1, xw, rows), lambda j, t: (j, 0, 0)),
                  pl.BlockSpec((1, rows, rows), lambda j, t: (j, 0, 0)),
                  pl.BlockSpec((1, 8, C), lambda j, t: (j, 0, 0)),
                  pl.BlockSpec((1, 8, C), lambda j, t: (j, 0, 0)),
                  pl.BlockSpec((1, 1, rows), lambda j, t: (j, 0, 0))],
        out_specs=pl.BlockSpec((Nc, rows), lambda j, t: (t, j)),
        scratch_shapes=[pltpu.VMEM((1, xw), F32)],
        compiler_params=_cparams(2, est),
        name="s5_chunk_scan",
    )(uf, E, Fm, A, apr, api, dflat)


def _glu_kernel(x_ref, w_ref, o_ref):
    zg = jnp.dot(x_ref[...], w_ref[...], preferred_element_type=F32)
    W = zg.shape[1] // 2
    o_ref[...] = (zg[:, :W] * jax.nn.sigmoid(zg[:, W:])).astype(o_ref.dtype)


def _glu(x, w, layer, *, bm):
    M, K = x.shape
    N2 = w.shape[2]
    est = 2 * (bm * K * 2 + K * N2 * 2 + bm * N2) + 3 * bm * N2 * 4
    return pl.pallas_call(
        _glu_kernel,
        out_shape=jax.ShapeDtypeStruct((M, N2 // 2), BF16),
        grid=(M // bm,),
        in_specs=[pl.BlockSpec((bm, K), lambda i: (i, 0)),
                  pl.BlockSpec((None, K, N2), lambda i: (layer, 0, 0))],
        out_specs=pl.BlockSpec((bm, N2 // 2), lambda i: (i, 0)),
        compiler_params=_cparams(1, est),
        name="s5_glu",
    )(x, w)


def _s5_chunked(z, s5p, d_skip, wglu, layer, *, off_u, width, L, Nc, bm):
    S = z.shape[0]
    sw = S5_SLICE_GROUPS * S5_GROUP
    nsl = width // sw
    u = z[:, off_u:off_u + width].reshape(S // L, L, nsl, sw)
    uf = u.transpose(0, 2, 1, 3).reshape(S // L, nsl * L * sw)
    yf = _s5c_scan(uf, s5p, d_skip, L=L, Nc=Nc)
    zc = yf.reshape(S // L, nsl, L, sw).transpose(0, 2, 1, 3).reshape(S, width)
    return _glu(zc, wglu, layer, bm=bm)


def _merge_kernel(gl_ref, od_ref, or_ref, os_ref, wg0_ref, wg1_ref, wg2_ref, wd_ref, wr_ref, ws_ref,
                  o_ref):
    gl = gl_ref[...]

    def branch(wg_ref, x_ref, w_ref):
        gate = jax.nn.sigmoid(jnp.dot(gl, wg_ref[...], preferred_element_type=F32))
        return gate * jnp.dot(x_ref[...], w_ref[...], preferred_element_type=F32)

    o_ref[...] = (branch(wg0_ref, od_ref, wd_ref) + branch(wg1_ref, or_ref, wr_ref)
                  + branch(wg2_ref, os_ref, ws_ref)).astype(o_ref.dtype)


def _merge(z, o_diff, o_ret, o_s5, wg, wd, wr, ws, layer, *, off_gate, bm, bn):
    S = z.shape[0]
    R = wg.shape[1]
    D = wd.shape[2]
    nb = D // bn
    kd, kr, ks = wd.shape[1], wr.shape[1], ws.shape[1]
    row = lambda k: pl.BlockSpec((bm, k), lambda i, j: (i, 0))
    col = lambda k: _wspec(layer, k, bn)
    est = 2 * (bm * (R + kd + kr + ks) * 2 + (3 * R + kd + kr + ks) * bn * 2 + bm * bn * 2) + 8 * bm * bn * 4
    return pl.pallas_call(
        _merge_kernel,
        out_shape=jax.ShapeDtypeStruct((S, D), BF16),
        grid=(S // bm, nb),
        in_specs=[pl.BlockSpec((bm, R), lambda i, j: (i, off_gate // R)),
                  row(kd), row(kr), row(ks),
                  _wspec(layer, R, bn),
                  _wspec(layer, R, bn, lambda j: nb + j),
                  _wspec(layer, R, bn, lambda j: 2 * nb + j),
                  col(kd), col(kr), col(ks)],
        out_specs=pl.BlockSpec((bm, bn), lambda i, j: (i, j)),
        compiler_params=_cparams(2, est),
        name="gated_merge",
    )(z, o_diff, o_ret, o_s5, wg, wg, wg, wd, wr, ws)


def _mm_res_kernel(x_ref, w_ref, h_ref, o_ref):
    o_ref[...] = h_ref[...] + jnp.dot(x_ref[...], w_ref[...], preferred_element_type=F32)


def _mm_res_norm_kernel(x_ref, w_ref, h_ref, g_ref, o_ref, hg_ref, ss_ref):
    hn = h_ref[...] + jnp.dot(x_ref[...], w_ref[...], preferred_element_type=F32)
    o_ref[...] = hn
    hg_ref[...] = (hn * g_ref[...]).astype(hg_ref.dtype)
    part = _lane_partial_sumsq(hn)

    @pl.when(pl.program_id(1) == 0)
    def _():
        ss_ref[...] = part

    @pl.when(pl.program_id(1) > 0)
    def _():
        ss_ref[...] += part


def _mm_res(x, w, layer, h, *, bm, bn, in_place, next_gain=None):
    M, K = x.shape
    N = w.shape[2]
    est = 2 * (bm * K * 2 + K * bn * 2 + 2 * bm * bn * 4 + bm * bn * 2 + bm * 512) + 2 * bm * bn * 4
    in_specs = [pl.BlockSpec((bm, K), lambda i, j: (i, 0)),
                _wspec(layer, K, bn),
                pl.BlockSpec((bm, bn), lambda i, j: (i, j))]
    h_spec = pl.BlockSpec((bm, bn), lambda i, j: (i, j))
    common = dict(grid=(M // bm, N // bn), input_output_aliases={2: 0} if in_place else {},
                  compiler_params=_cparams(2, est))
    if next_gain is None:
        return pl.pallas_call(
            _mm_res_kernel, out_shape=jax.ShapeDtypeStruct((M, N), F32), in_specs=in_specs,
            out_specs=h_spec, name="mm_residual", **common)(x, w, h)
    return pl.pallas_call(
        _mm_res_norm_kernel,
        out_shape=(jax.ShapeDtypeStruct((M, N), F32), jax.ShapeDtypeStruct((M, N), BF16),
                   jax.ShapeDtypeStruct((M, 128), F32)),
        in_specs=in_specs + [pl.BlockSpec((1, bn), lambda i, j: (0, j))],
        out_specs=(h_spec, pl.BlockSpec((bm, bn), lambda i, j: (i, j)),
                   pl.BlockSpec((bm, 128), lambda i, j: (i, 0))),
        name="mm_residual_norm", **common)(x, w, h, next_gain.reshape(1, N))


def _ffn_up_kernel(hg_ref, ss_ref, wg_ref, wu_ref, o_ref):
    hg = hg_ref[...]
    r = _row_rsqrt(ss_ref, hg_ref.shape[1])
    a = jnp.dot(hg, wg_ref[...], preferred_element_type=F32) * r
    b = jnp.dot(hg, wu_ref[...], preferred_element_type=F32) * r
    o_ref[...] = (jax.nn.silu(a) * b).astype(o_ref.dtype)


def _ffn_up(hg, ss, wg, wu, layer, *, bm, bn):
    M, K = hg.shape
    N = wg.shape[2]
    est = 2 * (bm * K * 2 + bm * 512 + 2 * K * bn * 2 + bm * bn * 2) + 5 * bm * bn * 4
    return pl.pallas_call(
        _ffn_up_kernel,
        out_shape=jax.ShapeDtypeStruct((M, N), BF16),
        grid=(M // bm, pl.cdiv(N, bn)),
        in_specs=[pl.BlockSpec((bm, K), lambda i, j: (i, 0)),
                  pl.BlockSpec((bm, 128), lambda i, j: (i, 0)),
                  _wspec(layer, K, bn),
                  _wspec(layer, K, bn)],
        out_specs=pl.BlockSpec((bm, bn), lambda i, j: (i, j)),
        compiler_params=_cparams(2, est),
        name="ffn_up",
    )(hg, ss, wg, wu)


def _ple_update(h_ref, g_ref, p_ref, wple_ref, wgd_ref, wgu_ref, hp_ref):
    _rms_rows(h_ref, g_ref, hp_ref, min(64, h_ref.shape[0]))
    t = jnp.dot(hp_ref[...], wgd_ref[...], preferred_element_type=F32).astype(BF16)
    gate = jax.nn.sigmoid(jnp.dot(t, wgu_ref[...], preferred_element_type=F32))
    e = jnp.dot(p_ref[...].astype(BF16), wple_ref[...], preferred_element_type=F32)
    return h_ref[...] + e * gate


def _ple_kernel(h_ref, g_ref, p_ref, wple_ref, wgd_ref, wgu_ref, gn_ref, o_ref, hg_ref, ss_ref, hp_ref):
    hn = _ple_update(h_ref, g_ref, p_ref, wple_ref, wgd_ref, wgu_ref, hp_ref)
    o_ref[...] = hn
    hg_ref[...] = (hn * gn_ref[...]).astype(hg_ref.dtype)
    ss_ref[...] = _lane_partial_sumsq(hn)


def _ple_final_kernel(h_ref, g_ref, p_ref, wple_ref, wgd_ref, wgu_ref, gn_ref, o_ref, hp_ref):
    hn = _ple_update(h_ref, g_ref, p_ref, wple_ref, wgd_ref, wgu_ref, hp_ref)
    ms = jnp.mean(hn * hn, axis=-1, keepdims=True)
    o_ref[...] = hn * lax.rsqrt(ms + EPS) * gn_ref[...]


def _ple(h, gain, p, wple, wgd, wgu, layer, next_gain, *, bm, final):
    M, D = h.shape
    P = p.shape[2]
    R = wgd.shape[2]
    est = 2 * (2 * bm * D * 4 + bm * D * 2 + bm * P * 4 + (P + 2 * R) * D * 2) + bm * D * 2 + 4 * bm * D * 4
    row = pl.BlockSpec((bm, D), lambda i: (i, 0))
    in_specs = [row,
                pl.BlockSpec((1, D), lambda i: (0, 0)),
                pl.BlockSpec((None, bm, P), lambda i: (layer, i, 0)),
                pl.BlockSpec((None, P, D), lambda i: (layer, 0, 0)),
                pl.BlockSpec((None, D, R), lambda i: (layer, 0, 0)),
                pl.BlockSpec((None, R, D), lambda i: (layer, 0, 0)),
                pl.BlockSpec((1, D), lambda i: (0, 0))]
    common = dict(grid=(M // bm,), in_specs=in_specs, scratch_shapes=[pltpu.VMEM((bm, D), BF16)],
                  input_output_aliases={0: 0}, compiler_params=_cparams(1, est))
    args = (h, gain.reshape(1, D), p, wple, wgd, wgu, next_gain.reshape(1, D))
    if final:
        return pl.pallas_call(_ple_final_kernel, out_shape=jax.ShapeDtypeStruct((M, D), F32),
                              out_specs=row, name="ple_final", **common)(*args)
    return pl.pallas_call(
        _ple_kernel,
        out_shape=(jax.ShapeDtypeStruct((M, D), F32), jax.ShapeDtypeStruct((M, D), BF16),
                   jax.ShapeDtypeStruct((M, 128), F32)),
        out_specs=(row, row, pl.BlockSpec((bm, 128), lambda i: (i, 0))),
        name="ple", **common)(*args)


def kernel(x, p, rel_bias, norm_mix, w_in, diff_lambda, diff_subln, s5_lambda_re, s5_lambda_im, s5_log_dt, s5_b_re, s5_b_im, s5_c_re, s5_c_im, s5_d, s5_w_glu, w_gate_up, w_br_diff, w_br_ret, w_br_s5, w_o, norm_ffn, w_ffn_gate, w_ffn_up, w_ffn_down, norm_ple, w_ple, w_ple_gate_down, w_ple_gate_up, norm_final):
    B, S, D = x.shape
    assert B == 1
    depth = w_in.shape[0]
    diff_w, ret_w, s5_w = w_br_diff.shape[1], w_br_ret.shape[1], w_br_s5.shape[1]
    gate_rank = w_gate_up.shape[1]
    diff_heads = diff_w // (2 * HEAD_DIM)
    ret_heads = ret_w // HEAD_DIM
    off_dq = 0
    off_dk = off_dq + diff_w
    off_dv = off_dk + diff_w
    off_rq = off_dv + diff_w
    off_rk = off_rq + ret_w
    off_rv = off_rk + ret_w
    off_rg = off_rv + ret_w
    off_su = off_rg + ret_w
    off_gate = off_su + s5_w
    in_w = off_gate + gate_rank
    assert w_in.shape[2] == in_w
    hidden = w_ffn_gate.shape[2]

    T_attn = _tile(S, 1024)
    T_ret = _tile(S, 256)
    bm_in = _tile(S, 1024)
    bn_in = _tile(in_w, 768) if in_w % 768 == 0 else _tile(in_w, 512)
    bm_mg, bn_mg = _tile(S, 1024), _tile(D, 512)
    bm_o, bn_o = _tile(S, 1024), _tile(D, 512)
    bm_up, bn_up = _tile(S, 1024), min(hidden, 512)
    bm_dn, bn_dn = _tile(S, 512), _tile(D, 512)
    bm_ple = _tile(S, 256)

    h = x.reshape(S, D)
    hg, ss = _row_prep(h, norm_mix[0], bm=bm_ple)
    bias_band = _bias_band(rel_bias, diff_heads, min(ATTN_ROW_CHUNK, T_attn))
    ret_tables = _retention_tables(S, ret_heads)
    z_scale = jnp.where(jnp.arange(in_w) < off_dk, HEAD_DIM ** -0.5 * LOG2E, 1.0).astype(F32)

    (w_in_b, w_glu_b, w_gate_b, w_brd_b, w_brr_b, w_brs_b, w_o_b, w_fg_b, w_fu_b, w_fd_b, w_ple_b,
     w_pgd_b, w_pgu_b) = (w.astype(BF16) for w in (
         w_in, s5_w_glu, w_gate_up, w_br_diff, w_br_ret, w_br_s5, w_o, w_ffn_gate, w_ffn_up,
         w_ffn_down, w_ple, w_ple_gate_down, w_ple_gate_up))
    p3 = p.reshape(depth, S, p.shape[-1])

    for i in range(depth):
        lam_init = 0.8 - 0.6 * math.exp(-0.3 * i)
        z = _norm_mm(hg, ss, w_in_b, i, z_scale, bm=bm_in, bn=bn_in)
        o_diff = _diff_attention(z, rel_bias, bias_band, diff_lambda[i], diff_subln[i], lam_init,
                                 off_q=off_dq, off_k=off_dk, off_v=off_dv, n_heads=diff_heads, T=T_attn)
        o_ret = _retention(z, ret_tables, off_q=off_rq, off_k=off_rk, off_v=off_rv, off_g=off_rg,
                           n_heads=ret_heads, TR=T_ret)
        s5p = _s5c_params(s5_lambda_re[i], s5_lambda_im[i], s5_log_dt[i], s5_b_re[i], s5_b_im[i],
                          s5_c_re[i], s5_c_im[i], S5_CHUNK)
        o_s5 = _s5_chunked(z, s5p, s5_d[i], w_glu_b, i, off_u=off_su, width=s5_w, L=S5_CHUNK,
                           Nc=_tile(S // S5_CHUNK, 256), bm=_tile(S, 1024))
        mixed = _merge(z, o_diff, o_ret, o_s5, w_gate_b, w_brd_b, w_brr_b, w_brs_b, i,
                       off_gate=off_gate, bm=bm_mg, bn=bn_mg)
        h, hg, ss = _mm_res(mixed, w_o_b, i, h, bm=bm_o, bn=bn_o, in_place=i > 0, next_gain=norm_ffn[i])
        act = _ffn_up(hg, ss, w_fg_b, w_fu_b, i, bm=bm_up, bn=bn_up)
        h = _mm_res(act, w_fd_b, i, h, bm=bm_dn, bn=bn_dn, in_place=True)
        if i + 1 < depth:
            h, hg, ss = _ple(h, norm_ple[i], p3, w_ple_b, w_pgd_b, w_pgu_b, i, norm_mix[i + 1],
                             bm=bm_ple, final=False)
        else:
            out = _ple(h, norm_ple[i], p3, w_ple_b, w_pgd_b, w_pgu_b, i, norm_final,
                       bm=bm_ple, final=True)
    return out.reshape(B, S, D)
```

```python
import functools
import math

import jax
import jax.numpy as jnp
from jax import lax
from jax.experimental import pallas as pl
from jax.experimental.pallas import tpu as pltpu

F32 = jnp.float32
BF16 = jnp.bfloat16
EPS = 1e-6

HEAD_DIM = 128
CHUNK = 128
REL_BUCKETS = 32
REL_MAX_DIST = 128
S5_GROUP = 16
S5_SLICE_GROUPS = 8
MASK_VALUE = -1e30
LOG2E = math.log2(math.e)
ATTN_ROW_CHUNK = 128
ATTN_HEADS_PER_STEP = 2

V7X_VMEM_REQUEST_CAP = 60 * 1024 * 1024
SPILL_AND_TEMP_ALLOWANCE = 8 * 1024 * 1024


def _cparams(n_axes, est_bytes):
    limit = int(min(est_bytes + SPILL_AND_TEMP_ALLOWANCE, V7X_VMEM_REQUEST_CAP))
    return pltpu.CompilerParams(dimension_semantics=("arbitrary",) * n_axes, vmem_limit_bytes=limit)


def _tile(n, want):
    t = min(n, want)
    while n % t:
        t //= 2
    return t


def _rms_rows(x_ref, g_ref, dst_ref, rows):
    n = x_ref.shape[0] // rows

    def body(i, c):
        r = pl.multiple_of(i * rows, rows)
        x = x_ref[pl.ds(r, rows), :]
        ms = jnp.mean(x * x, axis=-1, keepdims=True)
        dst_ref[pl.ds(r, rows), :] = (x * lax.rsqrt(ms + EPS) * g_ref[...]).astype(dst_ref.dtype)
        return c

    lax.fori_loop(0, n, body, 0)


def _lane_partial_sumsq(x):
    sq = x * x
    part = sq[:, 0:128]
    for j in range(1, x.shape[1] // 128):
        part = part + sq[:, j * 128:(j + 1) * 128]
    return part


def _row_rsqrt(ss_ref, n_cols):
    return lax.rsqrt(jnp.sum(ss_ref[...], axis=-1, keepdims=True) * (1.0 / n_cols) + EPS)


def _row_prep_kernel(x_ref, g_ref, hg_ref, ss_ref):
    x = x_ref[...]
    hg_ref[...] = (x * g_ref[...]).astype(hg_ref.dtype)
    ss_ref[...] = _lane_partial_sumsq(x)


def _row_prep(x, gain, *, bm):
    M, D = x.shape
    return pl.pallas_call(
        _row_prep_kernel,
        out_shape=(jax.ShapeDtypeStruct((M, D), BF16), jax.ShapeDtypeStruct((M, 128), F32)),
        grid=(M // bm,),
        in_specs=[pl.BlockSpec((bm, D), lambda i: (i, 0)), pl.BlockSpec((1, D), lambda i: (0, 0))],
        out_specs=(pl.BlockSpec((bm, D), lambda i: (i, 0)), pl.BlockSpec((bm, 128), lambda i: (i, 0))),
        compiler_params=_cparams(1, 2 * (bm * D * 6 + bm * 512) + 2 * bm * D * 4),
        name="row_prep",
    )(x, gain.reshape(1, D))


def _norm_mm_kernel(hg_ref, ss_ref, w_ref, cs_ref, o_ref):
    acc = jnp.dot(hg_ref[...], w_ref[...], preferred_element_type=F32)
    r = _row_rsqrt(ss_ref, hg_ref.shape[1])
    o_ref[...] = ((acc * r) * cs_ref[...]).astype(o_ref.dtype)


def _wspec(layer, k, bn, col_block=lambda j: j):
    return pl.BlockSpec((None, k, bn), lambda i, j: (layer, 0, col_block(j)))


def _norm_mm(hg, ss, w, layer, col_scale, *, bm, bn, out_dtype=BF16):
    M, K = hg.shape
    N = w.shape[2]
    est = 2 * (bm * K * 2 + bm * 512 + K * bn * 2 + bm * bn * 2) + 4 * bm * bn * 4
    return pl.pallas_call(
        _norm_mm_kernel,
        out_shape=jax.ShapeDtypeStruct((M, N), out_dtype),
        grid=(M // bm, N // bn),
        in_specs=[pl.BlockSpec((bm, K), lambda i, j: (i, 0)),
                  pl.BlockSpec((bm, 128), lambda i, j: (i, 0)),
                  _wspec(layer, K, bn),
                  pl.BlockSpec((1, bn), lambda i, j: (0, j))],
        out_specs=pl.BlockSpec((bm, bn), lambda i, j: (i, j)),
        compiler_params=_cparams(2, est),
        name="norm_mm",
    )(hg, ss, w, col_scale.reshape(1, N))


def _bias_band_kernel(rb_ref, o_ref):
    h = pl.program_id(0)
    R = o_ref.shape[1]
    r = lax.broadcasted_iota(jnp.int32, (R, 2 * R), 0)
    c = lax.broadcasted_iota(jnp.int32, (R, 2 * R), 1)
    rel = R + r - c
    n = jnp.maximum(rel, 0)
    max_exact = REL_BUCKETS // 2
    nf = jnp.maximum(n, 1).astype(F32)
    large = max_exact + (jnp.log(nf / max_exact) / math.log(REL_MAX_DIST / max_exact)
                         * (REL_BUCKETS - max_exact)).astype(jnp.int32)
    large = jnp.minimum(large, REL_BUCKETS - 1)
    bucket = jnp.where(n < max_exact, n, large)
    val = jnp.zeros((R, 2 * R), F32)
    for b in range(REL_BUCKETS):
        val = jnp.where(bucket == b, rb_ref[b, h], val)
    far = rb_ref[REL_BUCKETS - 1, h]
    o_ref[0] = jnp.where(rel >= 0, (val - far) * LOG2E, MASK_VALUE)


def _bias_band(rel_bias, n_heads, R):
    assert R >= REL_MAX_DIST
    return pl.pallas_call(
        _bias_band_kernel,
        out_shape=jax.ShapeDtypeStruct((n_heads, R, 2 * R), F32),
        grid=(n_heads,),
        in_specs=[pl.BlockSpec(memory_space=pltpu.SMEM)],
        out_specs=pl.BlockSpec((1, R, 2 * R), lambda h: (h, 0, 0)),
        compiler_params=_cparams(1, 24 * R * R * 4),
        name="t5_bias_band",
    )(rel_bias)


def _attn_kernel(qt_ref, kt_ref, rb_ref, li_ref, dl_ref, sub_ref, q_ref, k_ref, v_ref, bt_ref,
                 o_ref, m_ref, l_ref, acc_ref, *, row_chunks):
    hg = pl.program_id(0)
    p = pl.program_id(1)
    qi = qt_ref[p]
    ki = kt_ref[p]
    T = q_ref.shape[0]
    W = 2 * HEAD_DIM
    hb = q_ref.shape[1] // W

    @pl.when(ki == 0)
    def _():
        m_ref[...] = jnp.full(m_ref.shape, MASK_VALUE, F32)
        l_ref[...] = jnp.zeros(l_ref.shape, F32)
        acc_ref[...] = jnp.zeros(acc_ref.shape, F32)

    R = T // row_chunks
    nb = R // 128

    def step(kind):
        for r in range(row_chunks):
            rows = slice(r * R, (r + 1) * R)
            ncol = (r + 1) * R if kind == "diag" else T
            ncb = ncol // 128
            if kind == "diag":
                band0 = ncb - 2 * nb
            elif kind == "near" and r == 0:
                band0 = ncb - nb
            else:
                band0 = None
            for hh in range(hb):
                v = v_ref[0:ncol, hh * W:(hh + 1) * W]
                far = rb_ref[REL_BUCKETS - 1, hg * hb + hh] * LOG2E
                for c in range(2):
                    pc = 2 * hh + c
                    col0 = hh * W + c * HEAD_DIM
                    qc = q_ref[rows, col0:col0 + HEAD_DIM]
                    kc = k_ref[0:ncol, col0:col0 + HEAD_DIM]
                    s = lax.dot_general(qc, kc, (((1,), (1,)), ((), ())), preferred_element_type=F32)
                    blocks = []
                    for j in range(ncb):
                        sj = s[:, j * 128:(j + 1) * 128]
                        if band0 is not None and j - band0 >= 0:
                            jb = j - band0
                            sj = sj + bt_ref[hh, :, jb * 128:(jb + 1) * 128]
                        blocks.append(sj)
                    blk_max = blocks[0]
                    for sj in blocks[1:]:
                        blk_max = jnp.maximum(blk_max, sj)
                    m_prev = m_ref[pc, rows, :]
                    m_new = jnp.maximum(m_prev, jnp.max(blk_max, axis=-1, keepdims=True) + far)
                    alpha = jnp.exp2(m_prev - m_new)
                    shift = m_new - far
                    l_part = alpha * l_ref[pc, rows, :]
                    ps = []
                    for sj in blocks:
                        pj = jnp.exp2(sj - shift)
                        l_part = l_part + pj
                        ps.append(pj.astype(BF16))
                    l_ref[pc, rows, :] = l_part
                    m_ref[pc, rows, :] = m_new
                    pv = jnp.dot(jnp.concatenate(ps, axis=1), v, preferred_element_type=F32)
                    acc_ref[pc, rows, :] = (jnp.concatenate([alpha] * (W // 128), axis=1)
                                            * acc_ref[pc, rows, :] + pv)

    @pl.when(qi - ki >= 2)
    def _():
        step("far")

    @pl.when(qi - ki == 1)
    def _():
        step("near")

    @pl.when(qi == ki)
    def _():
        step("diag")
        lam_init = li_ref[0]
        dl = dl_ref[...]
        lam = (jnp.exp(jnp.sum(dl[0:1] * dl[1:2], axis=-1, keepdims=True))
               - jnp.exp(jnp.sum(dl[2:3] * dl[3:4], axis=-1, keepdims=True)) + lam_init)
        for hh in range(hb):
            l0 = jnp.sum(l_ref[2 * hh], axis=-1, keepdims=True)
            l1 = jnp.sum(l_ref[2 * hh + 1], axis=-1, keepdims=True)
            o = acc_ref[2 * hh] / l0 - lam * (acc_ref[2 * hh + 1] / l1)
            ms = jnp.mean(o * o, axis=-1, keepdims=True)
            y = (o * lax.rsqrt(ms + EPS) * sub_ref[...]) * (1.0 - lam_init)
            o_ref[:, hh * W:(hh + 1) * W] = y.astype(o_ref.dtype)


def _diff_attention(z, rel_bias, bias_band, diff_lambda, subln, lam_init, *, off_q, off_k, off_v,
                    n_heads, T):
    S = z.shape[0]
    W = 2 * HEAD_DIM
    R = bias_band.shape[1]
    nq = S // T
    qs, ks = [], []
    for qi in range(nq):
        for ki in range(qi + 1):
            qs.append(qi)
            ks.append(ki)
    qt = jnp.asarray(qs, jnp.int32)
    kt = jnp.asarray(ks, jnp.int32)
    hb = ATTN_HEADS_PER_STEP if n_heads % ATTN_HEADS_PER_STEP == 0 else 1
    WB = hb * W
    assert off_q % WB == 0 and off_k % WB == 0 and off_v % WB == 0
    bq, bk, bv = off_q // WB, off_k // WB, off_v // WB
    grid_spec = pltpu.PrefetchScalarGridSpec(
        num_scalar_prefetch=2,
        grid=(n_heads // hb, len(qs)),
        in_specs=[pl.BlockSpec(memory_space=pltpu.SMEM),
                  pl.BlockSpec(memory_space=pltpu.SMEM),
                  pl.BlockSpec((4, HEAD_DIM), lambda h, p, qt, kt: (0, 0)),
                  pl.BlockSpec((1, W), lambda h, p, qt, kt: (0, 0)),
                  pl.BlockSpec((T, WB), lambda h, p, qt, kt: (qt[p], bq + h)),
                  pl.BlockSpec((T, WB), lambda h, p, qt, kt: (kt[p], bk + h)),
                  pl.BlockSpec((T, WB), lambda h, p, qt, kt: (kt[p], bv + h)),
                  pl.BlockSpec((hb, R, 2 * R), lambda h, p, qt, kt: (h, 0, 0))],
        out_specs=pl.BlockSpec((T, WB), lambda h, p, qt, kt: (qt[p], h)),
        scratch_shapes=[pltpu.VMEM((2 * hb, T, 128), F32), pltpu.VMEM((2 * hb, T, 128), F32),
                        pltpu.VMEM((2 * hb, T, W), F32)],
    )
    est = (2 * (4 * T * WB * 2 + hb * 2 * R * R * 4) + 2 * hb * T * W * 4 + 4 * hb * T * 128 * 4
           + 16 * R * T * 4)
    return pl.pallas_call(
        functools.partial(_attn_kernel, row_chunks=T // R),
        out_shape=jax.ShapeDtypeStruct((S, n_heads * W), BF16),
        grid_spec=grid_spec,
        compiler_params=_cparams(2, est),
        name="diff_attention",
    )(qt, kt, rel_bias, jnp.full((1,), lam_init, F32), diff_lambda, subln.reshape(1, W),
      z, z, z, bias_band)


def _ret_kernel(cd_ref, q_ref, k_ref, v_ref, g_ref, cos_ref, sin_ref, dm_ref, xi_ref, ze_ref,
                o_ref, R_ref):
    t = pl.program_id(0)
    n_heads = R_ref.shape[0]

    @pl.when(t == 0)
    def _():
        R_ref[...] = jnp.zeros(R_ref.shape, F32)

    half = HEAD_DIM // 2
    Rs = [R_ref[h] for h in range(n_heads)]
    for c in range(q_ref.shape[0] // CHUNK):
        sl = slice(c * CHUNK, (c + 1) * CHUNK)
        cs = cos_ref[sl, :]
        sn = sin_ref[sl, :]
        for h in range(n_heads):
            hc = slice(h * HEAD_DIM, (h + 1) * HEAD_DIM)
            q = q_ref[sl, hc].astype(F32)
            k = k_ref[sl, hc].astype(F32)
            v = v_ref[sl, hc]
            qr = q * cs + pltpu.roll(q, half, 1) * sn
            kr = (k * cs + pltpu.roll(k, half, 1) * sn) * (HEAD_DIM ** -0.5)
            qb = qr.astype(BF16)
            inner = lax.dot_general(qb, kr.astype(BF16), (((1,), (1,)), ((), ())),
                                    preferred_element_type=F32) * dm_ref[h]
            o = (jnp.dot(inner.astype(BF16), v, preferred_element_type=F32)
                 + jnp.dot(qb, Rs[h].astype(BF16), preferred_element_type=F32) * xi_ref[h])
            Rs[h] = Rs[h] * cd_ref[h] + lax.dot_general(
                (kr * ze_ref[h]).astype(BF16), v, (((0,), (0,)), ((), ())), preferred_element_type=F32)
            o = o * lax.rsqrt(jnp.mean(o * o, axis=-1, keepdims=True) + EPS)
            g = g_ref[sl, hc].astype(F32)
            o_ref[sl, hc] = (jax.nn.silu(g) * o).astype(o_ref.dtype)
    for h in range(n_heads):
        R_ref[h] = Rs[h]


def _retention_tables(S, n_heads):
    half = HEAD_DIM // 2
    pos = jnp.arange(S, dtype=F32)
    theta = 1.0 / (10000.0 ** jnp.linspace(0.0, 1.0, half, dtype=F32))
    ang = pos[:, None] * theta[None, :]
    cos = jnp.cos(ang)
    sin = jnp.sin(ang)
    cosf = jnp.concatenate([cos, cos], axis=-1)
    sinf = jnp.concatenate([-sin, sin], axis=-1)
    log_gamma = jnp.log1p(-jnp.exp2(-5.0 - jnp.arange(n_heads, dtype=F32)))
    idx = jnp.arange(CHUNK)
    diff = idx[:, None] - idx[None, :]
    dmat = jnp.where(diff >= 0,
                     jnp.exp(log_gamma[:, None, None] * jnp.maximum(diff, 0).astype(F32)), 0.0)
    xi = jnp.exp(log_gamma[:, None] * (idx[None, :] + 1).astype(F32))[..., None]
    zeta = jnp.exp(log_gamma[:, None] * (CHUNK - 1 - idx)[None, :].astype(F32))[..., None]
    chunk_decay = jnp.exp(log_gamma * CHUNK)
    return cosf, sinf, dmat, xi, zeta, chunk_decay


def _retention(z, tables, *, off_q, off_k, off_v, off_g, n_heads, TR):
    S = z.shape[0]
    cosf, sinf, dmat, xi, zeta, chunk_decay = tables
    WR = n_heads * HEAD_DIM
    assert all(o % WR == 0 for o in (off_q, off_k, off_v, off_g))
    blk = lambda off: pl.BlockSpec((TR, WR), lambda t: (t, off // WR))
    tab = pl.BlockSpec((TR, HEAD_DIM), lambda t: (t, 0))
    whole = lambda last: pl.BlockSpec((n_heads, CHUNK, last), lambda t: (0, 0, 0))
    est = (2 * (5 * TR * WR * 2 + 2 * TR * HEAD_DIM * 4 + n_heads * 3 * CHUNK * CHUNK * 4)
           + n_heads * 16 * CHUNK * CHUNK * 4)
    return pl.pallas_call(
        _ret_kernel,
        out_shape=jax.ShapeDtypeStruct((S, WR), BF16),
        grid=(S // TR,),
        in_specs=[pl.BlockSpec(memory_space=pltpu.SMEM),
                  blk(off_q), blk(off_k), blk(off_v), blk(off_g), tab, tab,
                  whole(CHUNK), whole(1), whole(1)],
        out_specs=pl.BlockSpec((TR, WR), lambda t: (t, 0)),
        scratch_shapes=[pltpu.VMEM((n_heads, HEAD_DIM, HEAD_DIM), F32)],
        compiler_params=_cparams(1, est),
        name="retention",
    )(chunk_decay, z, z, z, z, cosf, sinf, dmat, xi, zeta)


def _s5_kernel(u_ref, benc_ref, apr_ref, api_ref, cdec_ref, d_ref, wglu_ref, o_ref,
               carry_ref, z_ref):
    t = pl.program_id(0)
    j = pl.program_id(1)
    nsl = pl.num_programs(1)
    Tt = u_ref.shape[0]
    C = apr_ref.shape[2]

    @pl.when((t == 0) & (j == 0))
    def _():
        carry_ref[...] = jnp.zeros(carry_ref.shape, F32)

    u = u_ref[...]
    x = jnp.dot(u, benc_ref[0], preferred_element_type=F32)
    ng = Tt // 8
    hr = x[:, :C].reshape(ng, 8, C)
    hi = x[:, C:].reshape(ng, 8, C)
    apr = apr_ref[0]
    api = api_ref[0]
    sub = lax.broadcasted_iota(jnp.int32, (ng, 8, C), 1)
    for d in (1, 2, 4):
        a_r = apr[d - 1:d, :].reshape(1, 1, C)
        a_i = api[d - 1:d, :].reshape(1, 1, C)
        sr = pltpu.roll(hr, d, 1)
        si = pltpu.roll(hi, d, 1)
        m = sub >= d
        nr = hr + jnp.where(m, a_r * sr - a_i * si, 0.0)
        ni = hi + jnp.where(m, a_r * si + a_i * sr, 0.0)
        hr, hi = nr, ni
    cr = carry_ref[j]
    c_r = cr[:, :C]
    c_i = cr[:, C:]
    out_r, out_i = [], []
    for g in range(ng):
        b_r = jnp.broadcast_to(c_r, (8, C))
        b_i = jnp.broadcast_to(c_i, (8, C))
        g_r = hr[g] + (apr * b_r - api * b_i)
        g_i = hi[g] + (apr * b_i + api * b_r)
        out_r.append(g_r)
        out_i.append(g_i)
        c_r = g_r[7:8, :]
        c_i = g_i[7:8, :]
    carry_ref[j] = jnp.concatenate([c_r, c_i], axis=1)
    hcat = jnp.concatenate([jnp.concatenate(out_r, axis=0), jnp.concatenate(out_i, axis=0)],
                           axis=1).astype(BF16)
    y = jnp.dot(hcat, cdec_ref[0], preferred_element_type=F32) + d_ref[0] * u.astype(F32)
    z_ref[j] = jax.nn.gelu(y).astype(BF16)

    @pl.when(j == nsl - 1)
    def _():
        zc = jnp.concatenate([z_ref[s] for s in range(z_ref.shape[0])], axis=1)
        zg = jnp.dot(zc, wglu_ref[...], preferred_element_type=F32)
        W = zg.shape[1] // 2
        o_ref[...] = (zg[:, :W] * jax.nn.sigmoid(zg[:, W:])).astype(o_ref.dtype)


def _s5_params(lam_re, lam_im, log_dt, b_re, b_im, c_re, c_im):
    G, N = lam_re.shape
    sg = S5_SLICE_GROUPS
    nsl = G // sg
    lr, li = lam_re.astype(F32), lam_im.astype(F32)
    dt = jnp.exp(log_dt.astype(F32))[:, None]
    mag = jnp.exp(lr * dt)
    ar = mag * jnp.cos(li * dt)
    ai = mag * jnp.sin(li * dt)
    den = lr * lr + li * li
    fr = ((ar - 1.0) * lr + ai * li) / den
    fi = (ai * lr - (ar - 1.0) * li) / den
    br_, bi_ = b_re.astype(F32), b_im.astype(F32)
    bbr = fr[..., None] * br_ - fi[..., None] * bi_
    bbi = fr[..., None] * bi_ + fi[..., None] * br_
    eye = jnp.eye(sg, dtype=F32)

    def enc(bb):
        return jnp.einsum('sgnj,gh->sgjhn', bb.reshape(nsl, sg, N, S5_GROUP), eye).reshape(
            nsl, sg * S5_GROUP, sg * N)

    def dec(cc):
        return jnp.einsum('sgjn,gh->sgnhj', cc.reshape(nsl, sg, S5_GROUP, N), eye).reshape(
            nsl, sg * N, sg * S5_GROUP)

    benc = jnp.concatenate([enc(bbr), enc(bbi)], axis=-1).astype(BF16)
    cdec = jnp.concatenate([dec(c_re.astype(F32)), -dec(c_im.astype(F32))], axis=1).astype(BF16)
    prs, pis = [ar], [ai]
    for _ in range(7):
        pr, pi = prs[-1], pis[-1]
        prs.append(pr * ar - pi * ai)
        pis.append(pr * ai + pi * ar)
    apr = jnp.stack(prs).reshape(8, nsl, sg * N).transpose(1, 0, 2)
    api = jnp.stack(pis).reshape(8, nsl, sg * N).transpose(1, 0, 2)
    return benc, cdec, apr, api


def _s5(z, s5p, d_skip, wglu, layer, *, off_u, width, Tt):
    S = z.shape[0]
    benc, cdec, apr, api = s5p
    nsl, uw, xw = benc.shape
    C = xw // 2
    lp = apr.shape[1]
    bu = off_u // uw
    est = (2 * (Tt * uw * 2 + uw * xw * 2 + 2 * lp * C * 4 + xw * uw * 2 + width * 2 * width * 2
                + Tt * width * 2) + nsl * Tt * uw * 2 + 24 * Tt * C * 4 + Tt * 2 * width * 4 * 2)
    return pl.pallas_call(
        _s5_kernel,
        out_shape=jax.ShapeDtypeStruct((S, width), BF16),
        grid=(S // Tt, nsl),
        in_specs=[pl.BlockSpec((Tt, uw), lambda t, j: (t, bu + j)),
                  pl.BlockSpec((1, uw, xw), lambda t, j: (j, 0, 0)),
                  pl.BlockSpec((1, lp, C), lambda t, j: (j, 0, 0)),
                  pl.BlockSpec((1, lp, C), lambda t, j: (j, 0, 0)),
                  pl.BlockSpec((1, xw, uw), lambda t, j: (j, 0, 0)),
                  pl.BlockSpec((1, 1, uw), lambda t, j: (j, 0, 0)),
                  pl.BlockSpec((None, width, 2 * width), lambda t, j: (layer, 0, 0))],
        out_specs=pl.BlockSpec((Tt, width), lambda t, j: (t, 0)),
        scratch_shapes=[pltpu.VMEM((nsl, 1, xw), F32), pltpu.VMEM((nsl, Tt, uw), BF16)],
        compiler_params=_cparams(2, est),
        name="s5",
    )(z, benc, apr, api, cdec, d_skip.astype(F32).reshape(nsl, 1, uw), wglu)


def _merge_kernel(gl_ref, od_ref, or_ref, os_ref, wg0_ref, wg1_ref, wg2_ref, wd_ref, wr_ref, ws_ref,
                  o_ref):
    gl = gl_ref[...]

    def branch(wg_ref, x_ref, w_ref):
        gate = jax.nn.sigmoid(jnp.dot(gl, wg_ref[...], preferred_element_type=F32))
        return gate * jnp.dot(x_ref[...], w_ref[...], preferred_element_type=F32)

    o_ref[...] = (branch(wg0_ref, od_ref, wd_ref) + branch(wg1_ref, or_ref, wr_ref)
                  + branch(wg2_ref, os_ref, ws_ref)).astype(o_ref.dtype)


def _merge(z, o_diff, o_ret, o_s5, wg, wd, wr, ws, layer, *, off_gate, bm, bn):
    S = z.shape[0]
    R = wg.shape[1]
    D = wd.shape[2]
    nb = D // bn
    kd, kr, ks = wd.shape[1], wr.shape[1], ws.shape[1]
    row = lambda k: pl.BlockSpec((bm, k), lambda i, j: (i, 0))
    col = lambda k: _wspec(layer, k, bn)
    est = 2 * (bm * (R + kd + kr + ks) * 2 + (3 * R + kd + kr + ks) * bn * 2 + bm * bn * 2) + 8 * bm * bn * 4
    return pl.pallas_call(
        _merge_kernel,
        out_shape=jax.ShapeDtypeStruct((S, D), BF16),
        grid=(S // bm, nb),
        in_specs=[pl.BlockSpec((bm, R), lambda i, j: (i, off_gate // R)),
                  row(kd), row(kr), row(ks),
                  _wspec(layer, R, bn),
                  _wspec(layer, R, bn, lambda j: nb + j),
                  _wspec(layer, R, bn, lambda j: 2 * nb + j),
                  col(kd), col(kr), col(ks)],
        out_specs=pl.BlockSpec((bm, bn), lambda i, j: (i, j)),
        compiler_params=_cparams(2, est),
        name="gated_merge",
    )(z, o_diff, o_ret, o_s5, wg, wg, wg, wd, wr, ws)


def _mm_res_kernel(x_ref, w_ref, h_ref, o_ref):
    o_ref[...] = h_ref[...] + jnp.dot(x_ref[...], w_ref[...], preferred_element_type=F32)


def _mm_res_norm_kernel(x_ref, w_ref, h_ref, g_ref, o_ref, hg_ref, ss_ref):
    hn = h_ref[...] + jnp.dot(x_ref[...], w_ref[...], preferred_element_type=F32)
    o_ref[...] = hn
    hg_ref[...] = (hn * g_ref[...]).astype(hg_ref.dtype)
    part = _lane_partial_sumsq(hn)

    @pl.when(pl.program_id(1) == 0)
    def _():
        ss_ref[...] = part

    @pl.when(pl.program_id(1) > 0)
    def _():
        ss_ref[...] += part


def _mm_res(x, w, layer, h, *, bm, bn, in_place, next_gain=None):
    M, K = x.shape
    N = w.shape[2]
    est = 2 * (bm * K * 2 + K * bn * 2 + 2 * bm * bn * 4 + bm * bn * 2 + bm * 512) + 2 * bm * bn * 4
    in_specs = [pl.BlockSpec((bm, K), lambda i, j: (i, 0)),
                _wspec(layer, K, bn),
                pl.BlockSpec((bm, bn), lambda i, j: (i, j))]
    h_spec = pl.BlockSpec((bm, bn), lambda i, j: (i, j))
    common = dict(grid=(M // bm, N // bn), input_output_aliases={2: 0} if in_place else {},
                  compiler_params=_cparams(2, est))
    if next_gain is None:
        return pl.pallas_call(
            _mm_res_kernel, out_shape=jax.ShapeDtypeStruct((M, N), F32), in_specs=in_specs,
            out_specs=h_spec, name="mm_residual", **common)(x, w, h)
    return pl.pallas_call(
        _mm_res_norm_kernel,
        out_shape=(jax.ShapeDtypeStruct((M, N), F32), jax.ShapeDtypeStruct((M, N), BF16),
                   jax.ShapeDtypeStruct((M, 128), F32)),
        in_specs=in_specs + [pl.BlockSpec((1, bn), lambda i, j: (0, j))],
        out_specs=(h_spec, pl.BlockSpec((bm, bn), lambda i, j: (i, j)),
                   pl.BlockSpec((bm, 128), lambda i, j: (i, 0))),
        name="mm_residual_norm", **common)(x, w, h, next_gain.reshape(1, N))


def _ffn_up_kernel(hg_ref, ss_ref, wg_ref, wu_ref, o_ref):
    hg = hg_ref[...]
    r = _row_rsqrt(ss_ref, hg_ref.shape[1])
    a = jnp.dot(hg, wg_ref[...], preferred_element_type=F32) * r
    b = jnp.dot(hg, wu_ref[...], preferred_element_type=F32) * r
    o_ref[...] = (jax.nn.silu(a) * b).astype(o_ref.dtype)


def _ffn_up(hg, ss, wg, wu, layer, *, bm, bn):
    M, K = hg.shape
    N = wg.shape[2]
    est = 2 * (bm * K * 2 + bm * 512 + 2 * K * bn * 2 + bm * bn * 2) + 5 * bm * bn * 4
    return pl.pallas_call(
        _ffn_up_kernel,
        out_shape=jax.ShapeDtypeStruct((M, N), BF16),
        grid=(M // bm, pl.cdiv(N, bn)),
        in_specs=[pl.BlockSpec((bm, K), lambda i, j: (i, 0)),
                  pl.BlockSpec((bm, 128), lambda i, j: (i, 0)),
                  _wspec(layer, K, bn),
                  _wspec(layer, K, bn)],
        out_specs=pl.BlockSpec((bm, bn), lambda i, j: (i, j)),
        compiler_params=_cparams(2, est),
        name="ffn_up",
    )(hg, ss, wg, wu)


def _ple_update(h_ref, g_ref, p_ref, wple_ref, wgd_ref, wgu_ref, hp_ref):
    _rms_rows(h_ref, g_ref, hp_ref, min(64, h_ref.shape[0]))
    t = jnp.dot(hp_ref[...], wgd_ref[...], preferred_element_type=F32).astype(BF16)
    gate = jax.nn.sigmoid(jnp.dot(t, wgu_ref[...], preferred_element_type=F32))
    e = jnp.dot(p_ref[...].astype(BF16), wple_ref[...], preferred_element_type=F32)
    return h_ref[...] + e * gate


def _ple_kernel(h_ref, g_ref, p_ref, wple_ref, wgd_ref, wgu_ref, gn_ref, o_ref, hg_ref, ss_ref, hp_ref):
    hn = _ple_update(h_ref, g_ref, p_ref, wple_ref, wgd_ref, wgu_ref, hp_ref)
    o_ref[...] = hn
    hg_ref[...] = (hn * gn_ref[...]).astype(hg_ref.dtype)
    ss_ref[...] = _lane_partial_sumsq(hn)


def _ple_final_kernel(h_ref, g_ref, p_ref, wple_ref, wgd_ref, wgu_ref, gn_ref, o_ref, hp_ref):
    hn = _ple_update(h_ref, g_ref, p_ref, wple_ref, wgd_ref, wgu_ref, hp_ref)
    ms = jnp.mean(hn * hn, axis=-1, keepdims=True)
    o_ref[...] = hn * lax.rsqrt(ms + EPS) * gn_ref[...]


def _ple(h, gain, p, wple, wgd, wgu, layer, next_gain, *, bm, final):
    M, D = h.shape
    P = p.shape[2]
    R = wgd.shape[2]
    est = 2 * (2 * bm * D * 4 + bm * D * 2 + bm * P * 4 + (P + 2 * R) * D * 2) + bm * D * 2 + 4 * bm * D * 4
    row = pl.BlockSpec((bm, D), lambda i: (i, 0))
    in_specs = [row,
                pl.BlockSpec((1, D), lambda i: (0, 0)),
                pl.BlockSpec((None, bm, P), lambda i: (layer, i, 0)),
                pl.BlockSpec((None, P, D), lambda i: (layer, 0, 0)),
                pl.BlockSpec((None, D, R), lambda i: (layer, 0, 0)),
                pl.BlockSpec((None, R, D), lambda i: (layer, 0, 0)),
                pl.BlockSpec((1, D), lambda i: (0, 0))]
    common = dict(grid=(M // bm,), in_specs=in_specs, scratch_shapes=[pltpu.VMEM((bm, D), BF16)],
                  input_output_aliases={0: 0}, compiler_params=_cparams(1, est))
    args = (h, gain.reshape(1, D), p, wple, wgd, wgu, next_gain.reshape(1, D))
    if final:
        return pl.pallas_call(_ple_final_kernel, out_shape=jax.ShapeDtypeStruct((M, D), F32),
                              out_specs=row, name="ple_final", **common)(*args)
    return pl.pallas_call(
        _ple_kernel,
        out_shape=(jax.ShapeDtypeStruct((M, D), F32), jax.ShapeDtypeStruct((M, D), BF16),
                   jax.ShapeDtypeStruct((M, 128), F32)),
        out_specs=(row, row, pl.BlockSpec((bm, 128), lambda i: (i, 0))),
        name="ple", **common)(*args)


def kernel(x, p, rel_bias, norm_mix, w_in, diff_lambda, diff_subln, s5_lambda_re, s5_lambda_im, s5_log_dt, s5_b_re, s5_b_im, s5_c_re, s5_c_im, s5_d, s5_w_glu, w_gate_up, w_br_diff, w_br_ret, w_br_s5, w_o, norm_ffn, w_ffn_gate, w_ffn_up, w_ffn_down, norm_ple, w_ple, w_ple_gate_down, w_ple_gate_up, norm_final):
    B, S, D = x.shape
    assert B == 1
    depth = w_in.shape[0]
    diff_w, ret_w, s5_w = w_br_diff.shape[1], w_br_ret.shape[1], w_br_s5.shape[1]
    gate_rank = w_gate_up.shape[1]
    diff_heads = diff_w // (2 * HEAD_DIM)
    ret_heads = ret_w // HEAD_DIM
    off_dq = 0
    off_dk = off_dq + diff_w
    off_dv = off_dk + diff_w
    off_rq = off_dv + diff_w
    off_rk = off_rq + ret_w
    off_rv = off_rk + ret_w
    off_rg = off_rv + ret_w
    off_su = off_rg + ret_w
    off_gate = off_su + s5_w
    in_w = off_gate + gate_rank
    assert w_in.shape[2] == in_w
    hidden = w_ffn_gate.shape[2]

    T_attn = _tile(S, 1024)
    T_ret = _tile(S, 256)
    T_s5 = _tile(S, 512)
    bm_in = _tile(S, 1024)
    bn_in = _tile(in_w, 768) if in_w % 768 == 0 else _tile(in_w, 512)
    bm_mg, bn_mg = _tile(S, 1024), _tile(D, 512)
    bm_o, bn_o = _tile(S, 1024), _tile(D, 512)
    bm_up, bn_up = _tile(S, 1024), min(hidden, 512)
    bm_dn, bn_dn = _tile(S, 512), _tile(D, 512)
    bm_ple = _tile(S, 256)

    h = x.reshape(S, D)
    hg, ss = _row_prep(h, norm_mix[0], bm=bm_ple)
    bias_band = _bias_band(rel_bias, diff_heads, min(ATTN_ROW_CHUNK, T_attn))
    ret_tables = _retention_tables(S, ret_heads)
    z_scale = jnp.where(jnp.arange(in_w) < off_dk, HEAD_DIM ** -0.5 * LOG2E, 1.0).astype(F32)

    (w_in_b, w_glu_b, w_gate_b, w_brd_b, w_brr_b, w_brs_b, w_o_b, w_fg_b, w_fu_b, w_fd_b, w_ple_b,
     w_pgd_b, w_pgu_b) = (w.astype(BF16) for w in (
         w_in, s5_w_glu, w_gate_up, w_br_diff, w_br_ret, w_br_s5, w_o, w_ffn_gate, w_ffn_up,
         w_ffn_down, w_ple, w_ple_gate_down, w_ple_gate_up))
    p3 = p.reshape(depth, S, p.shape[-1])

    for i in range(depth):
        lam_init = 0.8 - 0.6 * math.exp(-0.3 * i)
        z = _norm_mm(hg, ss, w_in_b, i, z_scale, bm=bm_in, bn=bn_in)
        o_diff = _diff_attention(z, rel_bias, bias_band, diff_lambda[i], diff_subln[i], lam_init,
                                 off_q=off_dq, off_k=off_dk, off_v=off_dv, n_heads=diff_heads, T=T_attn)
        o_ret = _retention(z, ret_tables, off_q=off_rq, off_k=off_rk, off_v=off_rv, off_g=off_rg,
                           n_heads=ret_heads, TR=T_ret)
        s5p = _s5_params(s5_lambda_re[i], s5_lambda_im[i], s5_log_dt[i], s5_b_re[i], s5_b_im[i],
                         s5_c_re[i], s5_c_im[i])
        o_s5 = _s5(z, s5p, s5_d[i], w_glu_b, i, off_u=off_su, width=s5_w, Tt=T_s5)
        mixed = _merge(z, o_diff, o_ret, o_s5, w_gate_b, w_brd_b, w_brr_b, w_brs_b, i,
                       off_gate=off_gate, bm=bm_mg, bn=bn_mg)
        h, hg, ss = _mm_res(mixed, w_o_b, i, h, bm=bm_o, bn=bn_o, in_place=i > 0, next_gain=norm_ffn[i])
        act = _ffn_up(hg, ss, w_fg_b, w_fu_b, i, bm=bm_up, bn=bn_up)
        h = _mm_res(act, w_fd_b, i, h, bm=bm_dn, bn=bn_dn, in_place=True)
        if i + 1 < depth:
            h, hg, ss = _ple(h, norm_ple[i], p3, w_ple_b, w_pgd_b, w_pgu_b, i, norm_mix[i + 1],
                             bm=bm_ple, final=False)
        else:
            out = _ple(h, norm_ple[i], p3, w_ple_b, w_pgd_b, w_pgu_b, i, norm_final,
                       bm=bm_ple, final=True)
    return out.reshape(B, S, D)
```

```python
import functools
import math

import jax
import jax.numpy as jnp
from jax import lax
from jax.experimental import pallas as pl
from jax.experimental.pallas import tpu as pltpu

F32 = jnp.float32
BF16 = jnp.bfloat16
EPS = 1e-6

HEAD_DIM = 128
CHUNK = 128
REL_BUCKETS = 32
REL_MAX_DIST = 128
S5_GROUP = 16
S5_SLICE_GROUPS = 8
MASK_VALUE = -1e30
LOG2E = math.log2(math.e)
ATTN_ROW_CHUNK = 128
ATTN_HEADS_PER_STEP = 2

V7X_VMEM_REQUEST_CAP = 60 * 1024 * 1024
SPILL_AND_TEMP_ALLOWANCE = 8 * 1024 * 1024


def _cparams(n_axes, est_bytes):
    limit = int(min(est_bytes + SPILL_AND_TEMP_ALLOWANCE, V7X_VMEM_REQUEST_CAP))
    return pltpu.CompilerParams(dimension_semantics=("arbitrary",) * n_axes, vmem_limit_bytes=limit)


def _tile(n, want):
    t = min(n, want)
    while n % t:
        t //= 2
    return t


def _rms_rows(x_ref, g_ref, dst_ref, rows):
    n = x_ref.shape[0] // rows

    def body(i, c):
        r = pl.multiple_of(i * rows, rows)
        x = x_ref[pl.ds(r, rows), :]
        ms = jnp.mean(x * x, axis=-1, keepdims=True)
        dst_ref[pl.ds(r, rows), :] = (x * lax.rsqrt(ms + EPS) * g_ref[...]).astype(dst_ref.dtype)
        return c

    lax.fori_loop(0, n, body, 0)


def _lane_partial_sumsq(x):
    sq = x * x
    part = sq[:, 0:128]
    for j in range(1, x.shape[1] // 128):
        part = part + sq[:, j * 128:(j + 1) * 128]
    return part


def _row_rsqrt(ss_ref, n_cols):
    return lax.rsqrt(jnp.sum(ss_ref[...], axis=-1, keepdims=True) * (1.0 / n_cols) + EPS)


def _row_prep_kernel(x_ref, g_ref, hg_ref, ss_ref):
    x = x_ref[...]
    hg_ref[...] = (x * g_ref[...]).astype(hg_ref.dtype)
    ss_ref[...] = _lane_partial_sumsq(x)


def _row_prep(x, gain, *, bm):
    M, D = x.shape
    return pl.pallas_call(
        _row_prep_kernel,
        out_shape=(jax.ShapeDtypeStruct((M, D), BF16), jax.ShapeDtypeStruct((M, 128), F32)),
        grid=(M // bm,),
        in_specs=[pl.BlockSpec((bm, D), lambda i: (i, 0)), pl.BlockSpec((1, D), lambda i: (0, 0))],
        out_specs=(pl.BlockSpec((bm, D), lambda i: (i, 0)), pl.BlockSpec((bm, 128), lambda i: (i, 0))),
        compiler_params=_cparams(1, 2 * (bm * D * 6 + bm * 512) + 2 * bm * D * 4),
        name="row_prep",
    )(x, gain.reshape(1, D))


def _norm_mm_kernel(hg_ref, ss_ref, w_ref, cs_ref, o_ref):
    acc = jnp.dot(hg_ref[...], w_ref[...], preferred_element_type=F32)
    r = _row_rsqrt(ss_ref, hg_ref.shape[1])
    o_ref[...] = ((acc * r) * cs_ref[...]).astype(o_ref.dtype)


def _wspec(layer, k, bn, col_block=lambda j: j):
    return pl.BlockSpec((None, k, bn), lambda i, j: (layer, 0, col_block(j)))


def _norm_mm(hg, ss, w, layer, col_scale, *, bm, bn, out_dtype=BF16):
    M, K = hg.shape
    N = w.shape[2]
    est = 2 * (bm * K * 2 + bm * 512 + K * bn * 2 + bm * bn * 2) + 4 * bm * bn * 4
    return pl.pallas_call(
        _norm_mm_kernel,
        out_shape=jax.ShapeDtypeStruct((M, N), out_dtype),
        grid=(M // bm, N // bn),
        in_specs=[pl.BlockSpec((bm, K), lambda i, j: (i, 0)),
                  pl.BlockSpec((bm, 128), lambda i, j: (i, 0)),
                  _wspec(layer, K, bn),
                  pl.BlockSpec((1, bn), lambda i, j: (0, j))],
        out_specs=pl.BlockSpec((bm, bn), lambda i, j: (i, j)),
        compiler_params=_cparams(2, est),
        name="norm_mm",
    )(hg, ss, w, col_scale.reshape(1, N))


def _bias_band_kernel(rb_ref, o_ref):
    h = pl.program_id(0)
    R = o_ref.shape[1]
    r = lax.broadcasted_iota(jnp.int32, (R, 2 * R), 0)
    c = lax.broadcasted_iota(jnp.int32, (R, 2 * R), 1)
    rel = R + r - c
    n = jnp.maximum(rel, 0)
    max_exact = REL_BUCKETS // 2
    nf = jnp.maximum(n, 1).astype(F32)
    large = max_exact + (jnp.log(nf / max_exact) / math.log(REL_MAX_DIST / max_exact)
                         * (REL_BUCKETS - max_exact)).astype(jnp.int32)
    large = jnp.minimum(large, REL_BUCKETS - 1)
    bucket = jnp.where(n < max_exact, n, large)
    val = jnp.zeros((R, 2 * R), F32)
    for b in range(REL_BUCKETS):
        val = jnp.where(bucket == b, rb_ref[b, h], val)
    far = rb_ref[REL_BUCKETS - 1, h]
    o_ref[0] = jnp.where(rel >= 0, (val - far) * LOG2E, MASK_VALUE)


def _bias_band(rel_bias, n_heads, R):
    assert R >= REL_MAX_DIST
    return pl.pallas_call(
        _bias_band_kernel,
        out_shape=jax.ShapeDtypeStruct((n_heads, R, 2 * R), F32),
        grid=(n_heads,),
        in_specs=[pl.BlockSpec(memory_space=pltpu.SMEM)],
        out_specs=pl.BlockSpec((1, R, 2 * R), lambda h: (h, 0, 0)),
        compiler_params=_cparams(1, 24 * R * R * 4),
        name="t5_bias_band",
    )(rel_bias)


def _attn_kernel(qt_ref, kt_ref, rb_ref, li_ref, dl_ref, sub_ref, q_ref, k_ref, v_ref, bt_ref,
                 o_ref, m_ref, l_ref, acc_ref, *, row_chunks):
    hg = pl.program_id(0)
    p = pl.program_id(1)
    qi = qt_ref[p]
    ki = kt_ref[p]
    T = q_ref.shape[0]
    W = 2 * HEAD_DIM
    hb = q_ref.shape[1] // W

    @pl.when(ki == 0)
    def _():
        m_ref[...] = jnp.full(m_ref.shape, MASK_VALUE, F32)
        l_ref[...] = jnp.zeros(l_ref.shape, F32)
        acc_ref[...] = jnp.zeros(acc_ref.shape, F32)

    R = T // row_chunks
    nb = R // 128

    def step(kind):
        for r in range(row_chunks):
            rows = slice(r * R, (r + 1) * R)
            ncol = (r + 1) * R if kind == "diag" else T
            ncb = ncol // 128
            if kind == "diag":
                band0 = ncb - 2 * nb
            elif kind == "near" and r == 0:
                band0 = ncb - nb
            else:
                band0 = None
            for hh in range(hb):
                v = v_ref[0:ncol, hh * W:(hh + 1) * W]
                far = rb_ref[REL_BUCKETS - 1, hg * hb + hh] * LOG2E
                for c in range(2):
                    pc = 2 * hh + c
                    col0 = hh * W + c * HEAD_DIM
                    qc = q_ref[rows, col0:col0 + HEAD_DIM]
                    kc = k_ref[0:ncol, col0:col0 + HEAD_DIM]
                    s = lax.dot_general(qc, kc, (((1,), (1,)), ((), ())), preferred_element_type=F32)
                    blocks = []
                    for j in range(ncb):
                        sj = s[:, j * 128:(j + 1) * 128]
                        if band0 is not None and j - band0 >= 0:
                            jb = j - band0
                            sj = sj + bt_ref[hh, :, jb * 128:(jb + 1) * 128]
                        blocks.append(sj)
                    blk_max = blocks[0]
                    for sj in blocks[1:]:
                        blk_max = jnp.maximum(blk_max, sj)
                    m_prev = m_ref[pc, rows, :]
                    m_new = jnp.maximum(m_prev, jnp.max(blk_max, axis=-1, keepdims=True) + far)
                    alpha = jnp.exp2(m_prev - m_new)
                    shift = m_new - far
                    l_part = alpha * l_ref[pc, rows, :]
                    ps = []
                    for sj in blocks:
                        pj = jnp.exp2(sj - shift)
                        l_part = l_part + pj
                        ps.append(pj.astype(BF16))
                    l_ref[pc, rows, :] = l_part
                    m_ref[pc, rows, :] = m_new
                    pv = jnp.dot(jnp.concatenate(ps, axis=1), v, preferred_element_type=F32)
                    acc_ref[pc, rows, :] = (jnp.concatenate([alpha] * (W // 128), axis=1)
                                            * acc_ref[pc, rows, :] + pv)

    @pl.when(qi - ki >= 2)
    def _():
        step("far")

    @pl.when(qi - ki == 1)
    def _():
        step("near")

    @pl.when(qi == ki)
    def _():
        step("diag")
        lam_init = li_ref[0]
        dl = dl_ref[...]
        lam = (jnp.exp(jnp.sum(dl[0:1] * dl[1:2], axis=-1, keepdims=True))
               - jnp.exp(jnp.sum(dl[2:3] * dl[3:4], axis=-1, keepdims=True)) + lam_init)
        for hh in range(hb):
            l0 = jnp.sum(l_ref[2 * hh], axis=-1, keepdims=True)
            l1 = jnp.sum(l_ref[2 * hh + 1], axis=-1, keepdims=True)
            o = acc_ref[2 * hh] / l0 - lam * (acc_ref[2 * hh + 1] / l1)
            ms = jnp.mean(o * o, axis=-1, keepdims=True)
            y = (o * lax.rsqrt(ms + EPS) * sub_ref[...]) * (1.0 - lam_init)
            o_ref[:, hh * W:(hh + 1) * W] = y.astype(o_ref.dtype)


def _diff_attention(z, rel_bias, bias_band, diff_lambda, subln, lam_init, *, off_q, off_k, off_v,
                    n_heads, T):
    S = z.shape[0]
    W = 2 * HEAD_DIM
    R = bias_band.shape[1]
    nq = S // T
    qs, ks = [], []
    for qi in range(nq):
        for ki in range(qi + 1):
            qs.append(qi)
            ks.append(ki)
    qt = jnp.asarray(qs, jnp.int32)
    kt = jnp.asarray(ks, jnp.int32)
    hb = ATTN_HEADS_PER_STEP if n_heads % ATTN_HEADS_PER_STEP == 0 else 1
    WB = hb * W
    assert off_q % WB == 0 and off_k % WB == 0 and off_v % WB == 0
    bq, bk, bv = off_q // WB, off_k // WB, off_v // WB
    grid_spec = pltpu.PrefetchScalarGridSpec(
        num_scalar_prefetch=2,
        grid=(n_heads // hb, len(qs)),
        in_specs=[pl.BlockSpec(memory_space=pltpu.SMEM),
                  pl.BlockSpec(memory_space=pltpu.SMEM),
                  pl.BlockSpec((4, HEAD_DIM), lambda h, p, qt, kt: (0, 0)),
                  pl.BlockSpec((1, W), lambda h, p, qt, kt: (0, 0)),
                  pl.BlockSpec((T, WB), lambda h, p, qt, kt: (qt[p], bq + h)),
                  pl.BlockSpec((T, WB), lambda h, p, qt, kt: (kt[p], bk + h)),
                  pl.BlockSpec((T, WB), lambda h, p, qt, kt: (kt[p], bv + h)),
                  pl.BlockSpec((hb, R, 2 * R), lambda h, p, qt, kt: (h, 0, 0))],
        out_specs=pl.BlockSpec((T, WB), lambda h, p, qt, kt: (qt[p], h)),
        scratch_shapes=[pltpu.VMEM((2 * hb, T, 128), F32), pltpu.VMEM((2 * hb, T, 128), F32),
                        pltpu.VMEM((2 * hb, T, W), F32)],
    )
    est = (2 * (4 * T * WB * 2 + hb * 2 * R * R * 4) + 2 * hb * T * W * 4 + 4 * hb * T * 128 * 4
           + 16 * R * T * 4)
    return pl.pallas_call(
        functools.partial(_attn_kernel, row_chunks=T // R),
        out_shape=jax.ShapeDtypeStruct((S, n_heads * W), BF16),
        grid_spec=grid_spec,
        compiler_params=_cparams(2, est),
        name="diff_attention",
    )(qt, kt, rel_bias, jnp.full((1,), lam_init, F32), diff_lambda, subln.reshape(1, W),
      z, z, z, bias_band)


def _ret_kernel(cd_ref, q_ref, k_ref, v_ref, g_ref, cos_ref, sin_ref, dm_ref, xi_ref, ze_ref,
                o_ref, R_ref):
    t = pl.program_id(0)
    n_heads = R_ref.shape[0]

    @pl.when(t == 0)
    def _():
        R_ref[...] = jnp.zeros(R_ref.shape, F32)

    half = HEAD_DIM // 2
    Rs = [R_ref[h] for h in range(n_heads)]
    for c in range(q_ref.shape[0] // CHUNK):
        sl = slice(c * CHUNK, (c + 1) * CHUNK)
        cs = cos_ref[sl, :]
        sn = sin_ref[sl, :]
        for h in range(n_heads):
            hc = slice(h * HEAD_DIM, (h + 1) * HEAD_DIM)
            q = q_ref[sl, hc].astype(F32)
            k = k_ref[sl, hc].astype(F32)
            v = v_ref[sl, hc]
            qr = q * cs + pltpu.roll(q, half, 1) * sn
            kr = (k * cs + pltpu.roll(k, half, 1) * sn) * (HEAD_DIM ** -0.5)
            qb = qr.astype(BF16)
            inner = lax.dot_general(qb, kr.astype(BF16), (((1,), (1,)), ((), ())),
                                    preferred_element_type=F32) * dm_ref[h]
            o = (jnp.dot(inner.astype(BF16), v, preferred_element_type=F32)
                 + jnp.dot(qb, Rs[h].astype(BF16), preferred_element_type=F32) * xi_ref[h])
            Rs[h] = Rs[h] * cd_ref[h] + lax.dot_general(
                (kr * ze_ref[h]).astype(BF16), v, (((0,), (0,)), ((), ())), preferred_element_type=F32)
            o = o * lax.rsqrt(jnp.mean(o * o, axis=-1, keepdims=True) + EPS)
            g = g_ref[sl, hc].astype(F32)
            o_ref[sl, hc] = (jax.nn.silu(g) * o).astype(o_ref.dtype)
    for h in range(n_heads):
        R_ref[h] = Rs[h]


def _retention_tables(S, n_heads):
    half = HEAD_DIM // 2
    pos = jnp.arange(S, dtype=F32)
    theta = 1.0 / (10000.0 ** jnp.linspace(0.0, 1.0, half, dtype=F32))
    ang = pos[:, None] * theta[None, :]
    cos = jnp.cos(ang)
    sin = jnp.sin(ang)
    cosf = jnp.concatenate([cos, cos], axis=-1)
    sinf = jnp.concatenate([-sin, sin], axis=-1)
    log_gamma = jnp.log1p(-jnp.exp2(-5.0 - jnp.arange(n_heads, dtype=F32)))
    idx = jnp.arange(CHUNK)
    diff = idx[:, None] - idx[None, :]
    dmat = jnp.where(diff >= 0,
                     jnp.exp(log_gamma[:, None, None] * jnp.maximum(diff, 0).astype(F32)), 0.0)
    xi = jnp.exp(log_gamma[:, None] * (idx[None, :] + 1).astype(F32))[..., None]
    zeta = jnp.exp(log_gamma[:, None] * (CHUNK - 1 - idx)[None, :].astype(F32))[..., None]
    chunk_decay = jnp.exp(log_gamma * CHUNK)
    return cosf, sinf, dmat, xi, zeta, chunk_decay


def _retention(z, tables, *, off_q, off_k, off_v, off_g, n_heads, TR):
    S = z.shape[0]
    cosf, sinf, dmat, xi, zeta, chunk_decay = tables
    WR = n_heads * HEAD_DIM
    assert all(o % WR == 0 for o in (off_q, off_k, off_v, off_g))
    blk = lambda off: pl.BlockSpec((TR, WR), lambda t: (t, off // WR))
    tab = pl.BlockSpec((TR, HEAD_DIM), lambda t: (t, 0))
    whole = lambda last: pl.BlockSpec((n_heads, CHUNK, last), lambda t: (0, 0, 0))
    est = (2 * (5 * TR * WR * 2 + 2 * TR * HEAD_DIM * 4 + n_heads * 3 * CHUNK * CHUNK * 4)
           + n_heads * 16 * CHUNK * CHUNK * 4)
    return pl.pallas_call(
        _ret_kernel,
        out_shape=jax.ShapeDtypeStruct((S, WR), BF16),
        grid=(S // TR,),
        in_specs=[pl.BlockSpec(memory_space=pltpu.SMEM),
                  blk(off_q), blk(off_k), blk(off_v), blk(off_g), tab, tab,
                  whole(CHUNK), whole(1), whole(1)],
        out_specs=pl.BlockSpec((TR, WR), lambda t: (t, 0)),
        scratch_shapes=[pltpu.VMEM((n_heads, HEAD_DIM, HEAD_DIM), F32)],
        compiler_params=_cparams(1, est),
        name="retention",
    )(chunk_decay, z, z, z, z, cosf, sinf, dmat, xi, zeta)


def _s5_kernel(u_ref, benc_ref, apr_ref, api_ref, cdec_ref, d_ref, wglu_ref, o_ref,
               carry_ref, z_ref):
    t = pl.program_id(0)
    j = pl.program_id(1)
    nsl = pl.num_programs(1)
    Tt = u_ref.shape[0]
    C = apr_ref.shape[2]

    @pl.when((t == 0) & (j == 0))
    def _():
        carry_ref[...] = jnp.zeros(carry_ref.shape, F32)

    u = u_ref[...]
    x = jnp.dot(u, benc_ref[0], preferred_element_type=F32)
    ng = Tt // 8
    hr = x[:, :C].reshape(ng, 8, C)
    hi = x[:, C:].reshape(ng, 8, C)
    apr = apr_ref[0]
    api = api_ref[0]
    sub = lax.broadcasted_iota(jnp.int32, (ng, 8, C), 1)
    for d in (1, 2, 4):
        a_r = apr[d - 1:d, :].reshape(1, 1, C)
        a_i = api[d - 1:d, :].reshape(1, 1, C)
        sr = pltpu.roll(hr, d, 1)
        si = pltpu.roll(hi, d, 1)
        m = sub >= d
        nr = hr + jnp.where(m, a_r * sr - a_i * si, 0.0)
        ni = hi + jnp.where(m, a_r * si + a_i * sr, 0.0)
        hr, hi = nr, ni
    cr = carry_ref[j]
    c_r = cr[:, :C]
    c_i = cr[:, C:]
    out_r, out_i = [], []
    for g in range(ng):
        b_r = jnp.broadcast_to(c_r, (8, C))
        b_i = jnp.broadcast_to(c_i, (8, C))
        g_r = hr[g] + (apr * b_r - api * b_i)
        g_i = hi[g] + (apr * b_i + api * b_r)
        out_r.append(g_r)
        out_i.append(g_i)
        c_r = g_r[7:8, :]
        c_i = g_i[7:8, :]
    carry_ref[j] = jnp.concatenate([c_r, c_i], axis=1)
    hcat = jnp.concatenate([jnp.concatenate(out_r, axis=0), jnp.concatenate(out_i, axis=0)],
                           axis=1).astype(BF16)
    y = jnp.dot(hcat, cdec_ref[0], preferred_element_type=F32) + d_ref[0] * u.astype(F32)
    z_ref[j] = jax.nn.gelu(y).astype(BF16)

    @pl.when(j == nsl - 1)
    def _():
        zc = jnp.concatenate([z_ref[s] for s in range(z_ref.shape[0])], axis=1)
        zg = jnp.dot(zc, wglu_ref[...], preferred_element_type=F32)
        W = zg.shape[1] // 2
        o_ref[...] = (zg[:, :W] * jax.nn.sigmoid(zg[:, W:])).astype(o_ref.dtype)


def _s5_params(lam_re, lam_im, log_dt, b_re, b_im, c_re, c_im):
    G, N = lam_re.shape
    sg = S5_SLICE_GROUPS
    nsl = G // sg
    lr, li = lam_re.astype(F32), lam_im.astype(F32)
    dt = jnp.exp(log_dt.astype(F32))[:, None]
    mag = jnp.exp(lr * dt)
    ar = mag * jnp.cos(li * dt)
    ai = mag * jnp.sin(li * dt)
    den = lr * lr + li * li
    fr = ((ar - 1.0) * lr + ai * li) / den
    fi = (ai * lr - (ar - 1.0) * li) / den
    br_, bi_ = b_re.astype(F32), b_im.astype(F32)
    bbr = fr[..., None] * br_ - fi[..., None] * bi_
    bbi = fr[..., None] * bi_ + fi[..., None] * br_
    eye = jnp.eye(sg, dtype=F32)

    def enc(bb):
        return jnp.einsum('sgnj,gh->sgjhn', bb.reshape(nsl, sg, N, S5_GROUP), eye).reshape(
            nsl, sg * S5_GROUP, sg * N)

    def dec(cc):
        return jnp.einsum('sgjn,gh->sgnhj', cc.reshape(nsl, sg, S5_GROUP, N), eye).reshape(
            nsl, sg * N, sg * S5_GROUP)

    benc = jnp.concatenate([enc(bbr), enc(bbi)], axis=-1).astype(BF16)
    cdec = jnp.concatenate([dec(c_re.astype(F32)), -dec(c_im.astype(F32))], axis=1).astype(BF16)
    prs, pis = [ar], [ai]
    for _ in range(7):
        pr, pi = prs[-1], pis[-1]
        prs.append(pr * ar - pi * ai)
        pis.append(pr * ai + pi * ar)
    apr = jnp.stack(prs).reshape(8, nsl, sg * N).transpose(1, 0, 2)
    api = jnp.stack(pis).reshape(8, nsl, sg * N).transpose(1, 0, 2)
    return benc, cdec, apr, api


def _s5(z, s5p, d_skip, wglu, layer, *, off_u, width, Tt):
    S = z.shape[0]
    benc, cdec, apr, api = s5p
    nsl, uw, xw = benc.shape
    C = xw // 2
    lp = apr.shape[1]
    bu = off_u // uw
    est = (2 * (Tt * uw * 2 + uw * xw * 2 + 2 * lp * C * 4 + xw * uw * 2 + width * 2 * width * 2
                + Tt * width * 2) + nsl * Tt * uw * 2 + 24 * Tt * C * 4 + Tt * 2 * width * 4 * 2)
    return pl.pallas_call(
        _s5_kernel,
        out_shape=jax.ShapeDtypeStruct((S, width), BF16),
        grid=(S // Tt, nsl),
        in_specs=[pl.BlockSpec((Tt, uw), lambda t, j: (t, bu + j)),
                  pl.BlockSpec((1, uw, xw), lambda t, j: (j, 0, 0)),
                  pl.BlockSpec((1, lp, C), lambda t, j: (j, 0, 0)),
                  pl.BlockSpec((1, lp, C), lambda t, j: (j, 0, 0)),
                  pl.BlockSpec((1, xw, uw), lambda t, j: (j, 0, 0)),
                  pl.BlockSpec((1, 1, uw), lambda t, j: (j, 0, 0)),
                  pl.BlockSpec((None, width, 2 * width), lambda t, j: (layer, 0, 0))],
        out_specs=pl.BlockSpec((Tt, width), lambda t, j: (t, 0)),
        scratch_shapes=[pltpu.VMEM((nsl, 1, xw), F32), pltpu.VMEM((nsl, Tt, uw), BF16)],
        compiler_params=_cparams(2, est),
        name="s5",
    )(z, benc, apr, api, cdec, d_skip.astype(F32).reshape(nsl, 1, uw), wglu)


def _merge_kernel(gl_ref, od_ref, or_ref, os_ref, wg0_ref, wg1_ref, wg2_ref, wd_ref, wr_ref, ws_ref,
                  o_ref):
    gl = gl_ref[...]

    def branch(wg_ref, x_ref, w_ref):
        gate = jax.nn.sigmoid(jnp.dot(gl, wg_ref[...], preferred_element_type=F32))
        return gate * jnp.dot(x_ref[...], w_ref[...], preferred_element_type=F32)

    o_ref[...] = (branch(wg0_ref, od_ref, wd_ref) + branch(wg1_ref, or_ref, wr_ref)
                  + branch(wg2_ref, os_ref, ws_ref)).astype(o_ref.dtype)


def _merge(z, o_diff, o_ret, o_s5, wg, wd, wr, ws, layer, *, off_gate, bm, bn):
    S = z.shape[0]
    R = wg.shape[1]
    D = wd.shape[2]
    nb = D // bn
    kd, kr, ks = wd.shape[1], wr.shape[1], ws.shape[1]
    row = lambda k: pl.BlockSpec((bm, k), lambda i, j: (i, 0))
    col = lambda k: _wspec(layer, k, bn)
    est = 2 * (bm * (R + kd + kr + ks) * 2 + (3 * R + kd + kr + ks) * bn * 2 + bm * bn * 2) + 8 * bm * bn * 4
    return pl.pallas_call(
        _merge_kernel,
        out_shape=jax.ShapeDtypeStruct((S, D), BF16),
        grid=(S // bm, nb),
        in_specs=[pl.BlockSpec((bm, R), lambda i, j: (i, off_gate // R)),
                  row(kd), row(kr), row(ks),
                  _wspec(layer, R, bn),
                  _wspec(layer, R, bn, lambda j: nb + j),
                  _wspec(layer, R, bn, lambda j: 2 * nb + j),
                  col(kd), col(kr), col(ks)],
        out_specs=pl.BlockSpec((bm, bn), lambda i, j: (i, j)),
        compiler_params=_cparams(2, est),
        name="gated_merge",
    )(z, o_diff, o_ret, o_s5, wg, wg, wg, wd, wr, ws)


def _mm_res_kernel(x_ref, w_ref, h_ref, o_ref):
    o_ref[...] = h_ref[...] + jnp.dot(x_ref[...], w_ref[...], preferred_element_type=F32)


def _mm_res_norm_kernel(x_ref, w_ref, h_ref, g_ref, o_ref, hg_ref, ss_ref):
    hn = h_ref[...] + jnp.dot(x_ref[...], w_ref[...], preferred_element_type=F32)
    o_ref[...] = hn
    hg_ref[...] = (hn * g_ref[...]).astype(hg_ref.dtype)
    part = _lane_partial_sumsq(hn)

    @pl.when(pl.program_id(1) == 0)
    def _():
        ss_ref[...] = part

    @pl.when(pl.program_id(1) > 0)
    def _():
        ss_ref[...] += part


def _mm_res(x, w, layer, h, *, bm, bn, in_place, next_gain=None):
    M, K = x.shape
    N = w.shape[2]
    est = 2 * (bm * K * 2 + K * bn * 2 + 2 * bm * bn * 4 + bm * bn * 2 + bm * 512) + 2 * bm * bn * 4
    in_specs = [pl.BlockSpec((bm, K), lambda i, j: (i, 0)),
                _wspec(layer, K, bn),
                pl.BlockSpec((bm, bn), lambda i, j: (i, j))]
    h_spec = pl.BlockSpec((bm, bn), lambda i, j: (i, j))
    common = dict(grid=(M // bm, N // bn), input_output_aliases={2: 0} if in_place else {},
                  compiler_params=_cparams(2, est))
    if next_gain is None:
        return pl.pallas_call(
            _mm_res_kernel, out_shape=jax.ShapeDtypeStruct((M, N), F32), in_specs=in_specs,
            out_specs=h_spec, name="mm_residual", **common)(x, w, h)
    return pl.pallas_call(
        _mm_res_norm_kernel,
        out_shape=(jax.ShapeDtypeStruct((M, N), F32), jax.ShapeDtypeStruct((M, N), BF16),
                   jax.ShapeDtypeStruct((M, 128), F32)),
        in_specs=in_specs + [pl.BlockSpec((1, bn), lambda i, j: (0, j))],
        out_specs=(h_spec, pl.BlockSpec((bm, bn), lambda i, j: (i, j)),
                   pl.BlockSpec((bm, 128), lambda i, j: (i, 0))),
        name="mm_residual_norm", **common)(x, w, h, next_gain.reshape(1, N))


def _ffn_up_kernel(hg_ref, ss_ref, wg_ref, wu_ref, o_ref):
    hg = hg_ref[...]
    r = _row_rsqrt(ss_ref, hg_ref.shape[1])
    a = jnp.dot(hg, wg_ref[...], preferred_element_type=F32) * r
    b = jnp.dot(hg, wu_ref[...], preferred_element_type=F32) * r
    o_ref[...] = (jax.nn.silu(a) * b).astype(o_ref.dtype)


def _ffn_up(hg, ss, wg, wu, layer, *, bm, bn):
    M, K = hg.shape
    N = wg.shape[2]
    est = 2 * (bm * K * 2 + bm * 512 + 2 * K * bn * 2 + bm * bn * 2) + 5 * bm * bn * 4
    return pl.pallas_call(
        _ffn_up_kernel,
        out_shape=jax.ShapeDtypeStruct((M, N), BF16),
        grid=(M // bm, pl.cdiv(N, bn)),
        in_specs=[pl.BlockSpec((bm, K), lambda i, j: (i, 0)),
                  pl.BlockSpec((bm, 128), lambda i, j: (i, 0)),
                  _wspec(layer, K, bn),
                  _wspec(layer, K, bn)],
        out_specs=pl.BlockSpec((bm, bn), lambda i, j: (i, j)),
        compiler_params=_cparams(2, est),
        name="ffn_up",
    )(hg, ss, wg, wu)


def _ple_update(h_ref, g_ref, p_ref, wple_ref, wgd_ref, wgu_ref, hp_ref):
    _rms_rows(h_ref, g_ref, hp_ref, min(64, h_ref.shape[0]))
    t = jnp.dot(hp_ref[...], wgd_ref[...], preferred_element_type=F32).astype(BF16)
    gate = jax.nn.sigmoid(jnp.dot(t, wgu_ref[...], preferred_element_type=F32))
    e = jnp.dot(p_ref[...].astype(BF16), wple_ref[...], preferred_element_type=F32)
    return h_ref[...] + e * gate


def _ple_kernel(h_ref, g_ref, p_ref, wple_ref, wgd_ref, wgu_ref, gn_ref, o_ref, hg_ref, ss_ref, hp_ref):
    hn = _ple_update(h_ref, g_ref, p_ref, wple_ref, wgd_ref, wgu_ref, hp_ref)
    o_ref[...] = hn
    hg_ref[...] = (hn * gn_ref[...]).astype(hg_ref.dtype)
    ss_ref[...] = _lane_partial_sumsq(hn)


def _ple_final_kernel(h_ref, g_ref, p_ref, wple_ref, wgd_ref, wgu_ref, gn_ref, o_ref, hp_ref):
    hn = _ple_update(h_ref, g_ref, p_ref, wple_ref, wgd_ref, wgu_ref, hp_ref)
    ms = jnp.mean(hn * hn, axis=-1, keepdims=True)
    o_ref[...] = hn * lax.rsqrt(ms + EPS) * gn_ref[...]


def _ple(h, gain, p, wple, wgd, wgu, layer, next_gain, *, bm, final):
    M, D = h.shape
    P = p.shape[2]
    R = wgd.shape[2]
    est = 2 * (2 * bm * D * 4 + bm * D * 2 + bm * P * 4 + (P + 2 * R) * D * 2) + bm * D * 2 + 4 * bm * D * 4
    row = pl.BlockSpec((bm, D), lambda i: (i, 0))
    in_specs = [row,
                pl.BlockSpec((1, D), lambda i: (0, 0)),
                pl.BlockSpec((None, bm, P), lambda i: (layer, i, 0)),
                pl.BlockSpec((None, P, D), lambda i: (layer, 0, 0)),
                pl.BlockSpec((None, D, R), lambda i: (layer, 0, 0)),
                pl.BlockSpec((None, R, D), lambda i: (layer, 0, 0)),
                pl.BlockSpec((1, D), lambda i: (0, 0))]
    common = dict(grid=(M // bm,), in_specs=in_specs, scratch_shapes=[pltpu.VMEM((bm, D), BF16)],
                  input_output_aliases={0: 0}, compiler_params=_cparams(1, est))
    args = (h, gain.reshape(1, D), p, wple, wgd, wgu, next_gain.reshape(1, D))
    if final:
        return pl.pallas_call(_ple_final_kernel, out_shape=jax.ShapeDtypeStruct((M, D), F32),
                              out_specs=row, name="ple_final", **common)(*args)
    return pl.pallas_call(
        _ple_kernel,
        out_shape=(jax.ShapeDtypeStruct((M, D), F32), jax.ShapeDtypeStruct((M, D), BF16),
                   jax.ShapeDtypeStruct((M, 128), F32)),
        out_specs=(row, row, pl.BlockSpec((bm, 128), lambda i: (i, 0))),
        name="ple", **common)(*args)


def kernel(x, p, rel_bias, norm_mix, w_in, diff_lambda, diff_subln, s5_lambda_re, s5_lambda_im, s5_log_dt, s5_b_re, s5_b_im, s5_c_re, s5_c_im, s5_d, s5_w_glu, w_gate_up, w_br_diff, w_br_ret, w_br_s5, w_o, norm_ffn, w_ffn_gate, w_ffn_up, w_ffn_down, norm_ple, w_ple, w_ple_gate_down, w_ple_gate_up, norm_final):
    B, S, D = x.shape
    assert B == 1
    depth = w_in.shape[0]
    diff_w, ret_w, s5_w = w_br_diff.shape[1], w_br_ret.shape[1], w_br_s5.shape[1]
    gate_rank = w_gate_up.shape[1]
    diff_heads = diff_w // (2 * HEAD_DIM)
    ret_heads = ret_w // HEAD_DIM
    off_dq = 0
    off_dk = off_dq + diff_w
    off_dv = off_dk + diff_w
    off_rq = off_dv + diff_w
    off_rk = off_rq + ret_w
    off_rv = off_rk + ret_w
    off_rg = off_rv + ret_w
    off_su = off_rg + ret_w
    off_gate = off_su + s5_w
    in_w = off_gate + gate_rank
    assert w_in.shape[2] == in_w
    hidden = w_ffn_gate.shape[2]

    T_attn = _tile(S, 1024)
    T_ret = _tile(S, 512)
    T_s5 = _tile(S, 1024)
    bm_in = _tile(S, 1024)
    bn_in = _tile(in_w, 768) if in_w % 768 == 0 else _tile(in_w, 512)
    bm_mg, bn_mg = _tile(S, 1024), _tile(D, 512)
    bm_o, bn_o = _tile(S, 1024), _tile(D, 512)
    bm_up, bn_up = _tile(S, 1024), min(hidden, 512)
    bm_dn, bn_dn = _tile(S, 512), _tile(D, 512)
    bm_ple = _tile(S, 256)

    h = x.reshape(S, D)
    hg, ss = _row_prep(h, norm_mix[0], bm=bm_ple)
    bias_band = _bias_band(rel_bias, diff_heads, min(ATTN_ROW_CHUNK, T_attn))
    ret_tables = _retention_tables(S, ret_heads)
    z_scale = jnp.where(jnp.arange(in_w) < off_dk, HEAD_DIM ** -0.5 * LOG2E, 1.0).astype(F32)

    (w_in_b, w_glu_b, w_gate_b, w_brd_b, w_brr_b, w_brs_b, w_o_b, w_fg_b, w_fu_b, w_fd_b, w_ple_b,
     w_pgd_b, w_pgu_b) = (w.astype(BF16) for w in (
         w_in, s5_w_glu, w_gate_up, w_br_diff, w_br_ret, w_br_s5, w_o, w_ffn_gate, w_ffn_up,
         w_ffn_down, w_ple, w_ple_gate_down, w_ple_gate_up))
    p3 = p.reshape(depth, S, p.shape[-1])

    for i in range(depth):
        lam_init = 0.8 - 0.6 * math.exp(-0.3 * i)
        z = _norm_mm(hg, ss, w_in_b, i, z_scale, bm=bm_in, bn=bn_in)
        o_diff = _diff_attention(z, rel_bias, bias_band, diff_lambda[i], diff_subln[i], lam_init,
                                 off_q=off_dq, off_k=off_dk, off_v=off_dv, n_heads=diff_heads, T=T_attn)
        o_ret = _retention(z, ret_tables, off_q=off_rq, off_k=off_rk, off_v=off_rv, off_g=off_rg,
                           n_heads=ret_heads, TR=T_ret)
        s5p = _s5_params(s5_lambda_re[i], s5_lambda_im[i], s5_log_dt[i], s5_b_re[i], s5_b_im[i],
                         s5_c_re[i], s5_c_im[i])
        o_s5 = _s5(z, s5p, s5_d[i], w_glu_b, i, off_u=off_su, width=s5_w, Tt=T_s5)
        mixed = _merge(z, o_diff, o_ret, o_s5, w_gate_b, w_brd_b, w_brr_b, w_brs_b, i,
                       off_gate=off_gate, bm=bm_mg, bn=bn_mg)
        h, hg, ss = _mm_res(mixed, w_o_b, i, h, bm=bm_o, bn=bn_o, in_place=i > 0, next_gain=norm_ffn[i])
        act = _ffn_up(hg, ss, w_fg_b, w_fu_b, i, bm=bm_up, bn=bn_up)
        h = _mm_res(act, w_fd_b, i, h, bm=bm_dn, bn=bn_dn, in_place=True)
        if i + 1 < depth:
            h, hg, ss = _ple(h, norm_ple[i], p3, w_ple_b, w_pgd_b, w_pgu_b, i, norm_mix[i + 1],
                             bm=bm_ple, final=False)
        else:
            out = _ple(h, norm_ple[i], p3, w_ple_b, w_pgd_b, w_pgu_b, i, norm_final,
                       bm=bm_ple, final=True)
    return out.reshape(B, S, D)
```
